```python
import math
import jax
import jax.numpy as jnp
from jax import lax
import numpy as np

D_MODEL = 4096
BATCH = 2
SEQ = 4096
DEPTH = 2

GRID_W = 64
CTX_LEN = 256
HEAD_DIM = 128
A_HEADS = 12
A_KV_HEADS = 4
A_GROUP = A_HEADS // A_KV_HEADS
B_HEADS = 12
NA_KH = 8
NA_KW = 16
C_HEADS = 8
C_QK_DIM = 64
C_V_DIM = 2 * C_QK_DIM
FFN_DIM = 7168
Q_BLOCK = 128
ROPE_THETA = 10000.0
EPS = 1e-6
N_MOD = 9

A_Q_W = A_HEADS * HEAD_DIM
A_KV_W = A_KV_HEADS * HEAD_DIM
B_W = B_HEADS * HEAD_DIM
C_QK_W = C_HEADS * 2 * C_QK_DIM
C_V_W = C_HEADS * C_V_DIM
IN_SIZES = (A_Q_W, A_KV_W, A_KV_W, B_W, B_W, B_W, C_QK_W, C_QK_W, C_V_W, 3 * D_MODEL)
IN_WIDTH = 22528

kernel_name = "hybrid_gated_dit_trunk"


def rms_norm(x, g):
    xf = x.astype(jnp.float32)
    y = xf * lax.rsqrt(jnp.mean(xf * xf, axis=-1, keepdims=True) + EPS)
    return (y * g.astype(jnp.float32)).astype(x.dtype)


def mod_norm(x, g, shift, scale):
    return rms_norm(x, g) * (1.0 + scale) + shift


def ada_mod(cvec, w_ada, b_ada):
    m = jax.nn.silu(cvec) @ w_ada + b_ada
    return m.reshape(cvec.shape[0], N_MOD, 1, D_MODEL)


def swiglu(h, w_in, w_out):
    gate, up = jnp.split(h @ w_in, 2, axis=-1)
    return (jax.nn.silu(gate) * up) @ w_out


def heads(t, n, d):
    return t.reshape(*t.shape[:-1], n, d)


def split_projection(p):
    offsets = tuple(int(o) for o in np.cumsum(IN_SIZES)[:-1])
    return jnp.split(p, offsets, axis=-1)


def axial_rope(n_tok, dim):
    t = jnp.arange(n_tok, dtype=jnp.int32)
    row = (t // GRID_W).astype(jnp.float32)
    col = (t % GRID_W).astype(jnp.float32)
    n_pairs = dim // 4
    inv = ROPE_THETA ** (-jnp.arange(n_pairs, dtype=jnp.float32) / n_pairs)
    ang = jnp.concatenate([row[:, None] * inv, col[:, None] * inv], axis=-1)
    return jnp.cos(ang), jnp.sin(ang)


def apply_rope(x, cos, sin):
    shape = (cos.shape[0],) + (1,) * (x.ndim - 3) + (cos.shape[1],)
    c, s = cos.reshape(shape), sin.reshape(shape)
    xf = x.astype(jnp.float32).reshape(*x.shape[:-1], -1, 2)
    x1, x2 = xf[..., 0], xf[..., 1]
    out = jnp.stack([x1 * c - x2 * s, x1 * s + x2 * c], axis=-1)
    return out.reshape(x.shape).astype(x.dtype)


def sweep_query_blocks(fn, q):
    b, s = q.shape[:2]
    nb = s // Q_BLOCK
    qb = jnp.moveaxis(q.reshape(b, nb, Q_BLOCK, *q.shape[2:]), 1, 0)
    ob = lax.map(fn, qb)
    return jnp.moveaxis(ob, 0, 1).reshape(b, s, *ob.shape[3:])


def gqa_core(q, k, v):
    s = jnp.einsum('bqhgd,bkhd->bhgqk', q, k, preferred_element_type=jnp.float32) * (q.shape[-1] ** -0.5)
    p = jax.nn.softmax(s, axis=-1).astype(v.dtype)
    o = jnp.einsum('bhgqk,bkhd->bqhgd', p, v)
    return o.reshape(*o.shape[:2], -1)


def diff_core(q, k, v, lam):
    s = jnp.einsum('bqhcd,bkhcd->bhcqk', q, k, preferred_element_type=jnp.float32) * (q.shape[-1] ** -0.5)
    p = jax.nn.softmax(s, axis=-1)
    a = (p[:, :, 0] - lam * p[:, :, 1]).astype(v.dtype)
    return jnp.einsum('bhqk,bkhd->bqhd', a, v)


def diff_heads_out(o, subln_g, lam_init):
    return (rms_norm(o, subln_g) * (1.0 - lam_init)).reshape(*o.shape[:2], -1)


def neighbourhood_attention(q, k, v, k_ctx, v_ctx, rel_bias, rows):
    b, s, h, d = q.shape
    kh, kw = min(NA_KH, rows), NA_KW
    qg = q.reshape(b, rows, GRID_W, h, d)
    kg = k.reshape(b, rows, GRID_W, h, d)
    vg = v.reshape(b, rows, GRID_W, h, d)
    cols = jnp.arange(GRID_W, dtype=jnp.int32)
    cs = jnp.clip(cols - kw // 2, 0, GRID_W - kw)
    col_idx = cs[:, None] + jnp.arange(kw, dtype=jnp.int32)[None]
    dc = col_idx - cols[:, None]
    scale = d ** -0.5
    n_win = kh * kw

    def row_block(r):
        rs = jnp.clip(r - kh // 2, 0, rows - kh)
        k_band = lax.dynamic_slice_in_dim(kg, rs, kh, axis=1)
        v_band = lax.dynamic_slice_in_dim(vg, rs, kh, axis=1)
        k_win = k_band[:, :, col_idx]
        v_win = v_band[:, :, col_idx]
        q_r = lax.dynamic_index_in_dim(qg, r, axis=1, keepdims=False)
        dr = rs + jnp.arange(kh, dtype=jnp.int32) - r
        bias = rel_bias[:, dr[:, None, None] + (NA_KH - 1), dc[None] + (NA_KW - 1)]
        bias = jnp.transpose(bias, (0, 2, 1, 3)).reshape(h, GRID_W, n_win)
        s_win = jnp.einsum('bchd,bicjhd->bhcij', q_r, k_win,
                           preferred_element_type=jnp.float32).reshape(b, h, GRID_W, n_win) * scale + bias
        s_ctx = jnp.einsum('bchd,blhd->bhcl', q_r, k_ctx, preferred_element_type=jnp.float32) * scale
        p = jax.nn.softmax(jnp.concatenate([s_win, s_ctx], axis=-1), axis=-1).astype(v.dtype)
        p_win = p[..., :n_win].reshape(b, h, GRID_W, kh, kw)
        p_ctx = p[..., n_win:]
        return (jnp.einsum('bhcij,bicjhd->bchd', p_win, v_win)
                + jnp.einsum('bhcl,blhd->bchd', p_ctx, v_ctx))

    o = lax.map(row_block, jnp.arange(rows, dtype=jnp.int32))
    return jnp.moveaxis(o, 0, 1).reshape(b, s, h * d)


def prep_a(qa, ka, va, gain, rope):
    q = rms_norm(heads(qa, A_HEADS, HEAD_DIM), gain[0])
    k = rms_norm(heads(ka, A_KV_HEADS, HEAD_DIM), gain[1])
    if rope is not None:
        q, k = apply_rope(q, *rope), apply_rope(k, *rope)
    q = q.reshape(*q.shape[:2], A_KV_HEADS, A_GROUP, HEAD_DIM)
    return q, k, heads(va, A_KV_HEADS, HEAD_DIM)


def prep_b(qb, kb, vb, gain):
    q = rms_norm(heads(qb, B_HEADS, HEAD_DIM), gain[0])
    k = rms_norm(heads(kb, B_HEADS, HEAD_DIM), gain[1])
    return q, k, heads(vb, B_HEADS, HEAD_DIM)


def prep_c(qc, kc, vc, gain, rope):
    q = rms_norm(qc.reshape(*qc.shape[:-1], C_HEADS, 2, C_QK_DIM), gain[0])
    k = rms_norm(kc.reshape(*kc.shape[:-1], C_HEADS, 2, C_QK_DIM), gain[1])
    if rope is not None:
        q, k = apply_rope(q, *rope), apply_rope(k, *rope)
    return q, k, heads(vc, C_HEADS, C_V_DIM)


def gated_merge(o_a, o_b, o_c, gates, w_br_a, w_br_b, w_br_c, w_out):
    g_a, g_b, g_c = jnp.split(jax.nn.sigmoid(gates), 3, axis=-1)
    m = g_a * (o_a @ w_br_a) + g_b * (o_b @ w_br_b) + g_c * (o_c @ w_br_c)
    return m @ w_out


def token_mixing(hx, hc, w_in, qk_gain_a, qk_gain_b, qk_gain_c, na_rel_bias, diff_lambda, diff_subln,
                 lam_init, w_br_a, w_br_b, w_br_c, w_out, rope_a, rope_c, rows, with_ctx_out):
    qa, ka, va, qb, kb, vb, qc, kc, vc, gx = split_projection(hx @ w_in)
    qa_c, ka_c, va_c, qb_c, kb_c, vb_c, qc_c, kc_c, vc_c, gc = split_projection(hc @ w_in)

    qa, ka, va = prep_a(qa, ka, va, qk_gain_a, rope_a)
    qa_c, ka_c, va_c = prep_a(qa_c, ka_c, va_c, qk_gain_a, None)
    ka_all = jnp.concatenate([ka, ka_c], axis=1)
    va_all = jnp.concatenate([va, va_c], axis=1)
    o_a = sweep_query_blocks(lambda q: gqa_core(q, ka_all, va_all), qa)

    qb, kb, vb = prep_b(qb, kb, vb, qk_gain_b)
    qb_c, kb_c, vb_c = prep_b(qb_c, kb_c, vb_c, qk_gain_b)
    o_b = neighbourhood_attention(qb, kb, vb, kb_c, vb_c, na_rel_bias, rows)

    lam = (jnp.exp(jnp.sum(diff_lambda[0] * diff_lambda[1]).astype(jnp.float32))
           - jnp.exp(jnp.sum(diff_lambda[2] * diff_lambda[3]).astype(jnp.float32)) + lam_init)
    qc, kc, vc = prep_c(qc, kc, vc, qk_gain_c, rope_c)
    qc_c, kc_c, vc_c = prep_c(qc_c, kc_c, vc_c, qk_gain_c, None)
    kc_all = jnp.concatenate([kc, kc_c], axis=1)
    vc_all = jnp.concatenate([vc, vc_c], axis=1)
    o_c = diff_heads_out(sweep_query_blocks(lambda q: diff_core(q, kc_all, vc_all, lam), qc), diff_subln, lam_init)

    y_lat = gated_merge(o_a, o_b, o_c, gx, w_br_a, w_br_b, w_br_c, w_out)
    if not with_ctx_out:
        return y_lat, None
    o_a_c = gqa_core(qa_c, ka_c, va_c)
    o_b_c = gqa_core(qb_c[:, :, :, None], kb_c, vb_c)
    o_c_c = diff_heads_out(diff_core(qc_c, kc_c, vc_c, lam), diff_subln, lam_init)
    y_ctx = gated_merge(o_a_c, o_b_c, o_c_c, gc, w_br_a, w_br_b, w_br_c, w_out)
    return y_lat, y_ctx


def setup_inputs(seed: int = 0) -> dict:
    key = jax.random.key(seed)
    ks = jax.random.split(key, 24)
    f32 = jnp.float32

    def nrm(k, shape, fan_in, gain=1.0):
        return jax.random.normal(k, shape, f32) * (gain * fan_in ** -0.5)

    def near_one(k, shape):
        return 1.0 + 0.05 * jax.random.normal(k, shape, f32)

    return {
        "x": jax.random.normal(ks[0], (BATCH, SEQ, D_MODEL), f32),
        "c": jax.random.normal(ks[1], (BATCH, D_MODEL), f32),
        "ctx": jax.random.normal(ks[2], (BATCH, CTX_LEN, D_MODEL), f32),
        "c_ctx": jax.random.normal(ks[3], (D_MODEL,), f32),
        "w_ada": nrm(ks[4], (DEPTH, D_MODEL, N_MOD * D_MODEL), D_MODEL, 0.5),
        "b_ada": 0.02 * jax.random.normal(ks[5], (DEPTH, N_MOD * D_MODEL), f32),
        "norm_g": near_one(ks[6], (DEPTH, 3, D_MODEL)),
        "ffn1_in": nrm(ks[7], (DEPTH, D_MODEL, 2 * FFN_DIM), D_MODEL),
        "ffn1_out": nrm(ks[8], (DEPTH, FFN_DIM, D_MODEL), FFN_DIM),
        "ffn2_in": nrm(ks[9], (DEPTH, D_MODEL, 2 * FFN_DIM), D_MODEL),
        "ffn2_out": nrm(ks[10], (DEPTH, FFN_DIM, D_MODEL), FFN_DIM),
        "w_in": nrm(ks[11], (DEPTH, D_MODEL, IN_WIDTH), D_MODEL),
        "qk_gain_a": near_one(ks[12], (DEPTH, 2, HEAD_DIM)),
        "qk_gain_b": near_one(ks[13], (DEPTH, 2, HEAD_DIM)),
        "qk_gain_c": near_one(ks[14], (DEPTH, 2, C_QK_DIM)),
        "na_rel_bias": 0.1 * jax.random.normal(ks[15], (DEPTH, B_HEADS, 2 * NA_KH - 1, 2 * NA_KW - 1), f32),
        "diff_lambda": 0.1 * jax.random.normal(ks[16], (DEPTH, 4, C_QK_DIM), f32),
        "diff_subln": near_one(ks[17], (DEPTH, C_V_DIM)),
        "w_br_a": nrm(ks[18], (DEPTH, A_Q_W, D_MODEL), A_Q_W),
        "w_br_b": nrm(ks[19], (DEPTH, B_W, D_MODEL), B_W),
        "w_br_c": nrm(ks[20], (DEPTH, C_V_W, D_MODEL), C_V_W),
        "w_out": nrm(ks[21], (DEPTH, D_MODEL, D_MODEL), D_MODEL),
    }


def reference(x, c, ctx, c_ctx, w_ada, b_ada, norm_g, ffn1_in, ffn1_out, ffn2_in, ffn2_out, w_in,
              qk_gain_a, qk_gain_b, qk_gain_c, na_rel_bias, diff_lambda, diff_subln,
              w_br_a, w_br_b, w_br_c, w_out):
    n_tok = x.shape[1]
    rows = n_tok // GRID_W
    rope_a = axial_rope(n_tok, HEAD_DIM)
    rope_c = axial_rope(n_tok, C_QK_DIM)
    cx = ctx
    for l in range(DEPTH):
        last = l == DEPTH - 1
        lam_init = 0.8 - 0.6 * math.exp(-0.3 * l)
        mx = ada_mod(c, w_ada[l], b_ada[l])
        mc = ada_mod(c_ctx[None], w_ada[l], b_ada[l])
        x = x + 0.5 * mx[:, 2] * swiglu(mod_norm(x, norm_g[l, 0], mx[:, 0], mx[:, 1]), ffn1_in[l], ffn1_out[l])
        cx = cx + 0.5 * mc[:, 2] * swiglu(mod_norm(cx, norm_g[l, 0], mc[:, 0], mc[:, 1]), ffn1_in[l], ffn1_out[l])
        hx = mod_norm(x, norm_g[l, 1], mx[:, 3], mx[:, 4])
        hc = mod_norm(cx, norm_g[l, 1], mc[:, 3], mc[:, 4])
        y_lat, y_ctx = token_mixing(hx, hc, w_in[l], qk_gain_a[l], qk_gain_b[l], qk_gain_c[l], na_rel_bias[l],
                                    diff_lambda[l], diff_subln[l], lam_init, w_br_a[l], w_br_b[l], w_br_c[l],
                                    w_out[l], rope_a, rope_c, rows, not last)
        x = x + mx[:, 5] * y_lat
        x = x + 0.5 * mx[:, 8] * swiglu(mod_norm(x, norm_g[l, 2], mx[:, 6], mx[:, 7]), ffn2_in[l], ffn2_out[l])
        if not last:
            cx = cx + mc[:, 5] * y_ctx
            cx = cx + 0.5 * mc[:, 8] * swiglu(mod_norm(cx, norm_g[l, 2], mc[:, 6], mc[:, 7]), ffn2_in[l], ffn2_out[l])
    return x
```

```python
import functools
import math

import numpy as np
import jax
import jax.numpy as jnp
from jax import lax
from jax.experimental import pallas as pl
from jax.experimental.pallas import tpu as pltpu

F32 = jnp.float32
BF16 = jnp.bfloat16

GRID_W = 64
HEAD_DIM = 128
C_QK_DIM = 64
NA_KH = 8
NA_KW = 16
ROPE_THETA = 10000.0
EPS = 1e-6
N_MOD = 9
LANES = 128
MOD_ROWS = 8
NEG_BIAS = -1e30

V7X_VMEM_BYTES = 64 * 1024 * 1024
VMEM_LIMIT = (V7X_VMEM_BYTES * 3) // 4

ROW_TILE = 512
Q_TILE = 256
NA_ROWS = Q_TILE // GRID_W


def _params(*sem):
    return pltpu.CompilerParams(dimension_semantics=sem, vmem_limit_bytes=VMEM_LIMIT)


def _sigmoid(v):
    return 1.0 / (1.0 + jnp.exp(-v))


def _dot(a, b):
    return jnp.dot(a, b, preferred_element_type=F32)


def _dot_t(a, b):
    return lax.dot_general(a, b, (((1,), (1,)), ((), ())), preferred_element_type=F32)


def _ada_kernel(c_ref, w_ref, b_ref, o_ref):
    c = c_ref[...]
    a = (c * _sigmoid(c)).astype(BF16)
    o_ref[0] = _dot(a, w_ref[0].astype(BF16)) + b_ref[0]


def ada_modulation(cvec, w_ada, b_ada, tn=512):
    depth, d, n = w_ada.shape
    return pl.pallas_call(
        _ada_kernel,
        grid=(depth, n // tn),
        in_specs=[
            pl.BlockSpec((MOD_ROWS, d), lambda l, j: (0, 0)),
            pl.BlockSpec((1, d, tn), lambda l, j: (l, 0, j)),
            pl.BlockSpec((1, 1, tn), lambda l, j: (l, 0, j)),
        ],
        out_specs=pl.BlockSpec((1, MOD_ROWS, tn), lambda l, j: (l, 0, j)),
        out_shape=jax.ShapeDtypeStruct((depth, MOD_ROWS, n), F32),
        compiler_params=_params("arbitrary", "arbitrary"),
        name="ada_mod",
    )(cvec, w_ada, b_ada.reshape(depth, 1, n))


def _normmod_kernel(x_ref, g_ref, shift_ref, scale_ref, o_ref):
    x = x_ref[...]
    y = x * lax.rsqrt(jnp.mean(x * x, axis=-1, keepdims=True) + EPS) * g_ref[...]
    o_ref[...] = (y * (1.0 + scale_ref[0]) + shift_ref[0]).astype(o_ref.dtype)


def norm_mod(x, g, mods, shift_idx, scale_idx, seg_of_tile, n_tiles, tr=256):
    d = x.shape[1]
    return pl.pallas_call(
        _normmod_kernel,
        grid=(n_tiles * (ROW_TILE // tr),),
        in_specs=[
            pl.BlockSpec((tr, d), lambda i: (i, 0)),
            pl.BlockSpec((1, d), lambda i: (0, 0)),
            pl.BlockSpec((1, 1, d), lambda i: (seg_of_tile(i * tr // ROW_TILE), 0, shift_idx)),
            pl.BlockSpec((1, 1, d), lambda i: (seg_of_tile(i * tr // ROW_TILE), 0, scale_idx)),
        ],
        out_specs=pl.BlockSpec((tr, d), lambda i: (i, 0)),
        out_shape=jax.ShapeDtypeStruct((n_tiles * ROW_TILE, d), BF16),
        compiler_params=_params("arbitrary"),
        name="norm_mod",
    )(x, g.reshape(1, d), mods, mods)


def _swiglu_kernel(a_ref, wg_ref, wu_ref, o_ref):
    a = a_ref[...]
    g = _dot(a, wg_ref[...])
    u = _dot(a, wu_ref[...])
    o_ref[...] = (g * _sigmoid(g) * u).astype(o_ref.dtype)


def ffn_up(h, w_in, n_tiles, tn=512):
    k = h.shape[1]
    f = w_in.shape[1] // 2
    nj = f // tn
    return pl.pallas_call(
        _swiglu_kernel,
        grid=(nj, n_tiles),
        in_specs=[
            pl.BlockSpec((ROW_TILE, k), lambda j, i: (i, 0)),
            pl.BlockSpec((k, tn), lambda j, i: (0, j)),
            pl.BlockSpec((k, tn), lambda j, i: (0, j + nj)),
        ],
        out_specs=pl.BlockSpec((ROW_TILE, tn), lambda j, i: (i, j)),
        out_shape=jax.ShapeDtypeStruct((n_tiles * ROW_TILE, f), BF16),
        compiler_params=_params("arbitrary", "arbitrary"),
        name="ffn_up",
    )(h, w_in, w_in)


def _resid_kernel(a_ref, w_ref, x_ref, gate_ref, o_ref, *, coef):
    y = _dot(a_ref[...], w_ref[...])
    o_ref[...] = x_ref[...] + (coef * gate_ref[0]) * y


def resid_matmul(a, w, x, mods, gate_idx, coef, seg_of_tile, n_tiles, tn):
    k, n = w.shape
    tn = min(tn, n)
    nj = n // tn
    return pl.pallas_call(
        functools.partial(_resid_kernel, coef=coef),
        grid=(nj, n_tiles),
        in_specs=[
            pl.BlockSpec((ROW_TILE, k), lambda j, i: (i, 0)),
            pl.BlockSpec((k, tn), lambda j, i: (0, j)),
            pl.BlockSpec((ROW_TILE, tn), lambda j, i: (i, j)),
            pl.BlockSpec((1, 1, tn), lambda j, i: (seg_of_tile(i), 0, gate_idx * nj + j)),
        ],
        out_specs=pl.BlockSpec((ROW_TILE, tn), lambda j, i: (i, j)),
        out_shape=jax.ShapeDtypeStruct((n_tiles * ROW_TILE, n), F32),
        compiler_params=_params("arbitrary", "arbitrary"),
        name="resid_matmul",
    )(a, w, x, mods)


def _proj_kernel(a_ref, w_ref, o_ref):
    o_ref[...] = _dot(a_ref[...], w_ref[...]).astype(o_ref.dtype)


def proj_plain(h, w, col0, width, n_tiles, tn=512):
    k = h.shape[1]
    off = col0 // tn
    return pl.pallas_call(
        _proj_kernel,
        grid=(width // tn, n_tiles),
        in_specs=[
            pl.BlockSpec((ROW_TILE, k), lambda j, i: (i, 0)),
            pl.BlockSpec((k, tn), lambda j, i: (0, j + off)),
        ],
        out_specs=pl.BlockSpec((ROW_TILE, tn), lambda j, i: (i, j)),
        out_shape=jax.ShapeDtypeStruct((n_tiles * ROW_TILE, width), BF16),
        compiler_params=_params("arbitrary", "arbitrary"),
        name="proj_plain",
    )(h, w)


def _swap_pairs(y):
    lane = lax.broadcasted_iota(jnp.int32, y.shape, 1)
    nxt = pltpu.roll(y, LANES - 1, 1)
    prv = pltpu.roll(y, 1, 1)
    return jnp.where(lane % 2 == 0, nxt, prv)


def _proj_qk_kernel(a_ref, w_ref, gain_ref, *rest, head_dim, rope):
    if rope:
        cos_ref, sin_ref, o_ref = rest
    else:
        (o_ref,) = rest
    acc = _dot(a_ref[...], w_ref[...])
    tn = acc.shape[1]
    for s in range(tn // LANES):
        sl = slice(s * LANES, (s + 1) * LANES)
        y = acc[:, sl]
        sq = y * y
        if head_dim == LANES:
            ms = jnp.mean(sq, axis=-1, keepdims=True)
        else:
            lane = lax.broadcasted_iota(jnp.int32, y.shape, 1)
            low = lane < head_dim
            s_low = jnp.sum(jnp.where(low, sq, 0.0), axis=-1, keepdims=True)
            s_high = jnp.sum(jnp.where(low, 0.0, sq), axis=-1, keepdims=True)
            ms = jnp.where(low, s_low, s_high) * (1.0 / head_dim)
        y = y * lax.rsqrt(ms + EPS) * gain_ref[:, sl]
        if rope:
            y = y * cos_ref[...] + _swap_pairs(y) * sin_ref[...]
        o_ref[:, sl] = y.astype(o_ref.dtype)


def proj_qk(h, w, col0, width, gain_row, head_dim, rope_tabs, n_tiles, tn=512):
    k = h.shape[1]
    off = col0 // tn
    rope = rope_tabs is not None
    in_specs = [
        pl.BlockSpec((ROW_TILE, k), lambda j, i: (i, 0)),
        pl.BlockSpec((k, tn), lambda j, i: (0, j + off)),
        pl.BlockSpec((1, tn), lambda j, i: (0, j)),
    ]
    args = [h, w, gain_row]
    if rope:
        in_specs += [pl.BlockSpec((ROW_TILE, LANES), lambda j, i: (i, 0))] * 2
        args += list(rope_tabs)
    return pl.pallas_call(
        functools.partial(_proj_qk_kernel, head_dim=head_dim, rope=rope),
        grid=(width // tn, n_tiles),
        in_specs=in_specs,
        out_specs=pl.BlockSpec((ROW_TILE, tn), lambda j, i: (i, j)),
        out_shape=jax.ShapeDtypeStruct((n_tiles * ROW_TILE, width), BF16),
        compiler_params=_params("arbitrary", "arbitrary"),
        name="proj_qk",
    )(*args)


def _merge_kernel(oa_ref, ob_ref, oc_ref, wa_ref, wb_ref, wc_ref, ga_ref, gb_ref, gc_ref, o_ref):
    m = _sigmoid(ga_ref[...].astype(F32)) * _dot(oa_ref[...], wa_ref[...])
    m += _sigmoid(gb_ref[...].astype(F32)) * _dot(ob_ref[...], wb_ref[...])
    m += _sigmoid(gc_ref[...].astype(F32)) * _dot(oc_ref[...], wc_ref[...])
    o_ref[...] = m.astype(o_ref.dtype)


def gated_merge(oa, ob, oc, wa, wb, wc, gates, n_tiles, tn=1024):
    d = wa.shape[1]
    tn = min(tn, d)
    nj = d // tn
    act = lambda o: pl.BlockSpec((ROW_TILE, o.shape[1]), lambda j, i: (i, 0))
    wgt = lambda w: pl.BlockSpec((w.shape[0], tn), lambda j, i: (0, j))
    gate = lambda b: pl.BlockSpec((ROW_TILE, tn), lambda j, i: (i, j + b * nj))
    return pl.pallas_call(
        _merge_kernel,
        grid=(nj, n_tiles),
        in_specs=[act(oa), act(ob), act(oc), wgt(wa), wgt(wb), wgt(wc), gate(0), gate(1), gate(2)],
        out_specs=pl.BlockSpec((ROW_TILE, tn), lambda j, i: (i, j)),
        out_shape=jax.ShapeDtypeStruct((n_tiles * ROW_TILE, d), BF16),
        compiler_params=_params("arbitrary", "arbitrary"),
        name="gated_merge",
    )(oa, ob, oc, wa, wb, wc, gates, gates, gates)


def _softmax2(s1, s2, v1, v2):
    m = jnp.max(s1, axis=-1, keepdims=True)
    if s2 is not None:
        m = jnp.maximum(m, jnp.max(s2, axis=-1, keepdims=True))
    p1 = jnp.exp(s1 - m)
    l = jnp.sum(p1, axis=-1, keepdims=True)
    o = _dot(p1.astype(v1.dtype), v1)
    if s2 is not None:
        p2 = jnp.exp(s2 - m)
        l = l + jnp.sum(p2, axis=-1, keepdims=True)
        o = o + _dot(p2.astype(v2.dtype), v2)
    return o / l


def _gqa_kernel(q_ref, kl_ref, kc_ref, vl_ref, vc_ref, o_ref, *, group, n_lat, with_ctx):
    qt = pl.program_id(2)

    def run(latent):
        for g in range(group):
            sl = slice(g * HEAD_DIM, (g + 1) * HEAD_DIM)
            q = q_ref[:, sl]
            s_ctx = _dot_t(q, kc_ref[...])
            if latent:
                o = _softmax2(_dot_t(q, kl_ref[...]), s_ctx, vl_ref[...], vc_ref[...])
            else:
                o = _softmax2(s_ctx, None, vc_ref[...], None)
            o_ref[:, sl] = o.astype(o_ref.dtype)

    if with_ctx:
        pl.when(qt < n_lat)(lambda: run(True))
        pl.when(qt >= n_lat)(lambda: run(False))
    else:
        run(True)


def _q_row_block(b, qt, n_lat, n_batch):
    return jnp.where(qt < n_lat, b * n_lat + qt, n_batch * n_lat + b)


def gqa_attention(q, k, v, n_batch, seq, ctx_len, group, with_ctx):
    rows = q.shape[0]
    kvh = k.shape[1] // HEAD_DIM
    n_lat = seq // Q_TILE
    ctx_blk0 = n_batch * seq // ctx_len
    gw = group * HEAD_DIM
    qmap = lambda b, h, t: (_q_row_block(b, t, n_lat, n_batch), h)
    lat = pl.BlockSpec((seq, HEAD_DIM), lambda b, h, t: (b, h))
    ctx = pl.BlockSpec((ctx_len, HEAD_DIM), lambda b, h, t: (ctx_blk0 + b, h))
    return pl.pallas_call(
        functools.partial(_gqa_kernel, group=group, n_lat=n_lat, with_ctx=with_ctx),
        grid=(n_batch, kvh, n_lat + int(with_ctx)),
        in_specs=[pl.BlockSpec((Q_TILE, gw), qmap), lat, ctx, lat, ctx],
        out_specs=pl.BlockSpec((Q_TILE, gw), qmap),
        out_shape=jax.ShapeDtypeStruct((rows, q.shape[1]), BF16),
        compiler_params=_params("arbitrary", "arbitrary", "arbitrary"),
        name="gqa_attention",
    )(q, k, k, v, v)


def _diff_kernel(q_ref, kl_ref, kc_ref, vl_ref, vc_ref, lam_ref, g_ref, o_ref, *, lam_init, n_lat, with_ctx):
    qt = pl.program_id(2)
    lp = lam_ref[...]
    lam = (jnp.exp(jnp.sum(lp[0:1] * lp[1:2], axis=-1, keepdims=True))
           - jnp.exp(jnp.sum(lp[2:3] * lp[3:4], axis=-1, keepdims=True)) + lam_init)

    def run(latent):
        q = q_ref[...]
        lane = lax.broadcasted_iota(jnp.int32, q.shape, 1)
        zero = jnp.zeros_like(q)
        outs = []
        for qm in (jnp.where(lane < C_QK_DIM, q, zero), jnp.where(lane < C_QK_DIM, zero, q)):
            s_ctx = _dot_t(qm, kc_ref[...])
            if latent:
                outs.append(_softmax2(_dot_t(qm, kl_ref[...]), s_ctx, vl_ref[...], vc_ref[...]))
            else:
                outs.append(_softmax2(s_ctx, None, vc_ref[...], None))
        o = outs[0] - lam * outs[1]
        o = o * lax.rsqrt(jnp.mean(o * o, axis=-1, keepdims=True) + EPS) * g_ref[...]
        o_ref[...] = (o * (1.0 - lam_init)).astype(o_ref.dtype)

    if with_ctx:
        pl.when(qt < n_lat)(lambda: run(True))
        pl.when(qt >= n_lat)(lambda: run(False))
    else:
        run(True)


def diff_attention(q, k, v, lam_params, subln, lam_init, n_batch, seq, ctx_len, with_ctx):
    rows = q.shape[0]
    n_heads = v.shape[1] // HEAD_DIM
    n_lat = seq // Q_TILE
    ctx_blk0 = n_batch * seq // ctx_len
    qmap = lambda b, h, t: (_q_row_block(b, t, n_lat, n_batch), h)
    lat = pl.BlockSpec((seq, HEAD_DIM), lambda b, h, t: (b, h))
    ctx = pl.BlockSpec((ctx_len, HEAD_DIM), lambda b, h, t: (ctx_blk0 + b, h))
    full = lambda a: pl.BlockSpec(a.shape, lambda b, h, t: (0,) * a.ndim)
    subln = subln.reshape(1, -1)
    return pl.pallas_call(
        functools.partial(_diff_kernel, lam_init=lam_init, n_lat=n_lat, with_ctx=with_ctx),
        grid=(n_batch, n_heads, n_lat + int(with_ctx)),
        in_specs=[pl.BlockSpec((Q_TILE, HEAD_DIM), qmap), lat, ctx, lat, ctx, full(lam_params), full(subln)],
        out_specs=pl.BlockSpec((Q_TILE, HEAD_DIM), qmap),
        out_shape=jax.ShapeDtypeStruct((rows, v.shape[1]), BF16),
        compiler_params=_params("arbitrary", "arbitrary", "arbitrary"),
        name="diff_attention",
    )(q, k, k, v, v, lam_params, subln)


def _na_plan(rows):
    kh = min(NA_KH, rows)
    band = min(NA_ROWS + kh, rows)
    patterns, types = [], []
    for blk in range(rows // NA_ROWS):
        r0 = blk * NA_ROWS
        bs = min(max(r0 - kh // 2, 0), rows - band)
        pat = []
        for qr in range(r0, r0 + NA_ROWS):
            rs = min(max(qr - kh // 2, 0), rows - kh)
            pat.append(tuple((kr - qr + kh - 1) if rs <= kr < rs + kh else -1 for kr in range(bs, bs + band)))
        pat = tuple(pat)
        if pat not in patterns:
            patterns.append(pat)
        types.append(patterns.index(pat))
    return band, np.asarray(types, np.int32), patterns


def _na_bias_kernel(rb_ref, o_ref, *, kh, patterns):
    h = pl.program_id(0)
    n_dc = 2 * NA_KW - 1
    base = h * ((2 * NA_KH - 1) * n_dc)
    qc = lax.broadcasted_iota(jnp.int32, (GRID_W, GRID_W), 0)
    kc = lax.broadcasted_iota(jnp.int32, (GRID_W, GRID_W), 1)
    dc = kc - qc
    cs = jnp.clip(qc - NA_KW // 2, 0, GRID_W - NA_KW)
    col_ok = (kc >= cs) & (kc < cs + NA_KW)
    neg = jnp.full((GRID_W, GRID_W), NEG_BIAS, F32)
    used = sorted({a for pat in patterns for row in pat for a in row if a >= 0})
    toep = {}
    for a in used:
        a_full = a + (NA_KH - kh)
        t = neg
        for b in range(n_dc):
            t = jnp.where(dc == b - (NA_KW - 1), rb_ref[base + a_full * n_dc + b], t)
        toep[a] = jnp.where(col_ok, t, neg)
    for t_id, pat in enumerate(patterns):
        for qr, row in enumerate(pat):
            blocks = [toep[a] if a >= 0 else neg for a in row]
            o_ref[0, t_id, qr * GRID_W:(qr + 1) * GRID_W, :] = jnp.concatenate(blocks, axis=1)


def na_bias_tables(rel_bias, rows):
    n_heads = rel_bias.shape[0]
    kh = min(NA_KH, rows)
    band, _, patterns = _na_plan(rows)
    shape = (n_heads, len(patterns), Q_TILE, band * GRID_W)
    return pl.pallas_call(
        functools.partial(_na_bias_kernel, kh=kh, patterns=patterns),
        grid=(n_heads,),
        in_specs=[pl.BlockSpec(memory_space=pltpu.SMEM)],
        out_specs=pl.BlockSpec((1,) + shape[1:], lambda h: (h, 0, 0, 0)),
        out_shape=jax.ShapeDtypeStruct(shape, F32),
        compiler_params=_params("arbitrary"),
        name="na_bias_tables",
    )(rel_bias.reshape(-1))


def _na_kernel(types_ref, q_ref, kl_ref, kc_ref, vl_ref, vc_ref, bias_ref, o_ref, *, rows, band, n_lat, with_ctx):
    del types_ref
    rt = pl.program_id(2)
    kh = min(NA_KH, rows)

    def latent():
        bs = jnp.clip(rt * NA_ROWS - kh // 2, 0, rows - band)
        start = pl.multiple_of(bs * GRID_W, GRID_W)
        q = q_ref[...]
        s_win = _dot_t(q, kl_ref[pl.ds(start, band * GRID_W), :]) + bias_ref[0, 0]
        s_ctx = _dot_t(q, kc_ref[...])
        o = _softmax2(s_win, s_ctx, vl_ref[pl.ds(start, band * GRID_W), :], vc_ref[...])
        o_ref[...] = o.astype(o_ref.dtype)

    def context():
        o = _softmax2(_dot_t(q_ref[...], kc_ref[...]), None, vc_ref[...], None)
        o_ref[...] = o.astype(o_ref.dtype)

    if with_ctx:
        pl.when(rt < n_lat)(latent)
        pl.when(rt >= n_lat)(context)
    else:
        latent()


def neighbourhood_attention(q, k, v, bias_tabs, n_batch, seq, ctx_len, with_ctx):
    rows_tok = q.shape[0]
    n_heads = q.shape[1] // HEAD_DIM
    rows = seq // GRID_W
    band, types, _ = _na_plan(rows)
    n_lat = seq // Q_TILE
    ctx_blk0 = n_batch * seq // ctx_len
    types = jnp.asarray(np.concatenate([types, types[-1:]]))
    qmap = lambda b, h, t, ty: (_q_row_block(b, t, n_lat, n_batch), h)
    lat = pl.BlockSpec((seq, HEAD_DIM), lambda b, h, t, ty: (b, h))
    ctx = pl.BlockSpec((ctx_len, HEAD_DIM), lambda b, h, t, ty: (ctx_blk0 + b, h))
    bias = pl.BlockSpec((1, 1, Q_TILE, band * GRID_W), lambda b, h, t, ty: (h, ty[t], 0, 0))
    return pl.pallas_call(
        functools.partial(_na_kernel, rows=rows, band=band, n_lat=n_lat, with_ctx=with_ctx),
        grid_spec=pltpu.PrefetchScalarGridSpec(
            num_scalar_prefetch=1,
            grid=(n_batch, n_heads, n_lat + int(with_ctx)),
            in_specs=[pl.BlockSpec((Q_TILE, HEAD_DIM), qmap), lat, ctx, lat, ctx, bias],
            out_specs=pl.BlockSpec((Q_TILE, HEAD_DIM), qmap),
        ),
        out_shape=jax.ShapeDtypeStruct((rows_tok, q.shape[1]), BF16),
        compiler_params=_params("arbitrary", "arbitrary", "arbitrary"),
        name="neighbourhood_attention",
    )(types, q, k, k, v, v, bias_tabs)


def _rope_tables(seq, dim, n_batch, n_ctx_rows):
    t = jnp.arange(seq, dtype=jnp.int32)
    row = (t // GRID_W).astype(F32)
    col = (t % GRID_W).astype(F32)
    n_pairs = dim // 4
    inv = ROPE_THETA ** (-jnp.arange(n_pairs, dtype=F32) / n_pairs)
    ang = jnp.concatenate([row[:, None] * inv, col[:, None] * inv], axis=-1)
    cos = jnp.repeat(jnp.cos(ang), 2, axis=-1)
    sin = jnp.stack([-jnp.sin(ang), jnp.sin(ang)], axis=-1).reshape(seq, dim)
    reps = LANES // dim
    cos, sin = jnp.tile(cos, (n_batch, reps)), jnp.tile(sin, (n_batch, reps))
    cos = jnp.concatenate([cos, jnp.ones((n_ctx_rows, LANES), F32)], axis=0)
    sin = jnp.concatenate([sin, jnp.zeros((n_ctx_rows, LANES), F32)], axis=0)
    return cos, sin


def kernel(x, c, ctx, c_ctx, w_ada, b_ada, norm_g, ffn1_in, ffn1_out, ffn2_in, ffn2_out, w_in,
           qk_gain_a, qk_gain_b, qk_gain_c, na_rel_bias, diff_lambda, diff_subln,
           w_br_a, w_br_b, w_br_c, w_out):
    n_batch, seq, d = x.shape
    ctx_len = ctx.shape[1]
    depth = w_ada.shape[0]
    a_q_w, b_w, c_w = w_br_a.shape[1], w_br_b.shape[1], w_br_c.shape[1]
    a_kv_w = (w_in.shape[2] - a_q_w - 3 * b_w - 3 * c_w - 3 * d) // 2
    group = a_q_w // a_kv_w
    assert ctx_len == Q_TILE and seq % ROW_TILE == 0 and (n_batch * ctx_len) % ROW_TILE == 0
    assert n_batch + 1 <= MOD_ROWS
    n_lat_rows = n_batch * seq
    n_ctx_rows = n_batch * ctx_len
    lat_tiles = n_lat_rows // ROW_TILE
    all_tiles = lat_tiles + n_ctx_rows // ROW_TILE
    tiles_per_batch = seq // ROW_TILE
    seg_of_tile = lambda i: jnp.minimum(i // tiles_per_batch, n_batch)

    sizes = (a_q_w, a_kv_w, a_kv_w, b_w, b_w, b_w, c_w, c_w, c_w, 3 * d)
    offs = [int(o) for o in np.cumsum((0,) + sizes)]

    cvec = jnp.concatenate([c, c_ctx[None], jnp.zeros((MOD_ROWS - n_batch - 1, d), F32)], axis=0)
    mods_all = ada_modulation(cvec, w_ada, b_ada)

    rope_a = _rope_tables(seq, HEAD_DIM, n_batch, n_ctx_rows)
    rope_c = _rope_tables(seq, C_QK_DIM, n_batch, n_ctx_rows)
    rows = seq // GRID_W

    xs = jnp.concatenate([x.reshape(n_lat_rows, d), ctx.reshape(n_ctx_rows, d)], axis=0)

    def gain_row(gain, width, scale):
        return jnp.tile(gain * scale, width // gain.shape[0]).reshape(1, width)

    for l in range(depth):
        last = l == depth - 1
        with_ctx = not last
        lam_init = 0.8 - 0.6 * math.exp(-0.3 * l)
        mods = mods_all[l].reshape(MOD_ROWS, 1, N_MOD * d)
        w1i, w1o = ffn1_in[l].astype(BF16), ffn1_out[l].astype(BF16)
        w2i, w2o = ffn2_in[l].astype(BF16), ffn2_out[l].astype(BF16)
        wp = w_in[l].astype(BF16)
        wa, wb, wc, wo = (w_br_a[l].astype(BF16), w_br_b[l].astype(BF16), w_br_c[l].astype(BF16),
                          w_out[l].astype(BF16))

        h = norm_mod(xs, norm_g[l, 0], mods, 0, 1, seg_of_tile, all_tiles)
        a = ffn_up(h, w1i, all_tiles)
        xs = resid_matmul(a, w1o, xs, mods, 2, 0.5, seg_of_tile, all_tiles, tn=512)

        h = norm_mod(xs, norm_g[l, 1], mods, 3, 4, seg_of_tile, all_tiles)
        sa, sc = HEAD_DIM ** -0.5, C_QK_DIM ** -0.5
        qa = proj_qk(h, wp, offs[0], a_q_w, gain_row(qk_gain_a[l, 0], a_q_w, sa), HEAD_DIM, rope_a, all_tiles)
        ka = proj_qk(h, wp, offs[1], a_kv_w, gain_row(qk_gain_a[l, 1], a_kv_w, 1.0), HEAD_DIM, rope_a, all_tiles)
        va = proj_plain(h, wp, offs[2], a_kv_w, all_tiles)
        qb = proj_qk(h, wp, offs[3], b_w, gain_row(qk_gain_b[l, 0], b_w, sa), HEAD_DIM, None, all_tiles)
        kb = proj_qk(h, wp, offs[4], b_w, gain_row(qk_gain_b[l, 1], b_w, 1.0), HEAD_DIM, None, all_tiles)
        vb = proj_plain(h, wp, offs[5], b_w, all_tiles)
        qc = proj_qk(h, wp, offs[6], c_w, gain_row(qk_gain_c[l, 0], c_w, sc), C_QK_DIM, rope_c, all_tiles)
        kc = proj_qk(h, wp, offs[7], c_w, gain_row(qk_gain_c[l, 1], c_w, 1.0), C_QK_DIM, rope_c, all_tiles)
        vc = proj_plain(h, wp, offs[8], c_w, all_tiles)
        mix_tiles = all_tiles if with_ctx else lat_tiles
        gates = proj_plain(h, wp, offs[9], 3 * d, mix_tiles)

        o_a = gqa_attention(qa, ka, va, n_batch, seq, ctx_len, group, with_ctx)
        bias_tabs = na_bias_tables(na_rel_bias[l], rows)
        o_b = neighbourhood_attention(qb, kb, vb, bias_tabs, n_batch, seq, ctx_len, with_ctx)
        o_c = diff_attention(qc, kc, vc, diff_lambda[l], diff_subln[l], lam_init, n_batch, seq, ctx_len, with_ctx)

        m = gated_merge(o_a, o_b, o_c, wa, wb, wc, gates, mix_tiles)
        xs = resid_matmul(m, wo, xs, mods, 5, 1.0, seg_of_tile, mix_tiles, tn=1024)

        h = norm_mod(xs, norm_g[l, 2], mods, 6, 7, seg_of_tile, mix_tiles)
        a = ffn_up(h, w2i, mix_tiles)
        xs = resid_matmul(a, w2o, xs, mods, 8, 0.5, seg_of_tile, mix_tiles, tn=512)

    return xs[:n_lat_rows].reshape(n_batch, seq, d)
```

```python
import functools
import math

import numpy as np
import jax
import jax.numpy as jnp
from jax import lax
from jax.experimental import pallas as pl
from jax.experimental.pallas import tpu as pltpu

F32 = jnp.float32
BF16 = jnp.bfloat16

GRID_W = 64
HEAD_DIM = 128
C_QK_DIM = 64
NA_KH = 8
NA_KW = 16
ROPE_THETA = 10000.0
EPS = 1e-6
N_MOD = 9
LANES = 128
MOD_ROWS = 8
NEG_BIAS = -1e30
LOG2E = math.log2(math.e)
KEY_CHUNK = 512

V7X_VMEM_BYTES = 64 * 1024 * 1024
VMEM_LIMIT = (V7X_VMEM_BYTES * 3) // 4

ROW_TILE = 512
QK_SUB_ROWS = 256
NA_HEADS = 2
Q_TILE = 256
NA_ROWS = Q_TILE // GRID_W


def _params(*sem):
    return pltpu.CompilerParams(dimension_semantics=sem, vmem_limit_bytes=VMEM_LIMIT)


def _sigmoid(v):
    return 1.0 / (1.0 + jnp.exp(-v))


def _dot(a, b):
    return jnp.dot(a, b, preferred_element_type=F32)


def _dot_t(a, b):
    return lax.dot_general(a, b, (((1,), (1,)), ((), ())), preferred_element_type=F32)


def _ada_kernel(c_ref, w_ref, b_ref, o_ref):
    c = c_ref[...]
    a = (c * _sigmoid(c)).astype(BF16)
    o_ref[0] = _dot(a, w_ref[0].astype(BF16)) + b_ref[0]


def ada_modulation(cvec, w_ada, b_ada, tn=512):
    depth, d, n = w_ada.shape
    return pl.pallas_call(
        _ada_kernel,
        grid=(depth, n // tn),
        in_specs=[
            pl.BlockSpec((MOD_ROWS, d), lambda l, j: (0, 0)),
            pl.BlockSpec((1, d, tn), lambda l, j: (l, 0, j)),
            pl.BlockSpec((1, 1, tn), lambda l, j: (l, 0, j)),
        ],
        out_specs=pl.BlockSpec((1, MOD_ROWS, tn), lambda l, j: (l, 0, j)),
        out_shape=jax.ShapeDtypeStruct((depth, MOD_ROWS, n), F32),
        compiler_params=_params("arbitrary", "arbitrary"),
        name="ada_mod",
    )(cvec, w_ada, b_ada.reshape(depth, 1, n))


def _normmod_kernel(x_ref, g_ref, shift_ref, scale_ref, o_ref):
    x = x_ref[...]
    y = x * lax.rsqrt(jnp.mean(x * x, axis=-1, keepdims=True) + EPS) * g_ref[...]
    o_ref[...] = (y * (1.0 + scale_ref[0]) + shift_ref[0]).astype(o_ref.dtype)


def norm_mod(x, g, mods, shift_idx, scale_idx, seg_of_tile, n_tiles, tr=256):
    d = x.shape[1]
    return pl.pallas_call(
        _normmod_kernel,
        grid=(n_tiles * (ROW_TILE // tr),),
        in_specs=[
            pl.BlockSpec((tr, d), lambda i: (i, 0)),
            pl.BlockSpec((1, d), lambda i: (0, 0)),
            pl.BlockSpec((1, 1, d), lambda i: (seg_of_tile(i * tr // ROW_TILE), 0, shift_idx)),
            pl.BlockSpec((1, 1, d), lambda i: (seg_of_tile(i * tr // ROW_TILE), 0, scale_idx)),
        ],
        out_specs=pl.BlockSpec((tr, d), lambda i: (i, 0)),
        out_shape=jax.ShapeDtypeStruct((n_tiles * ROW_TILE, d), BF16),
        compiler_params=_params("arbitrary"),
        name="norm_mod",
    )(x, g.reshape(1, d), mods, mods)


def _swiglu_kernel(a_ref, wg_ref, wu_ref, o_ref):
    a = a_ref[...]
    g = _dot(a, wg_ref[...])
    u = _dot(a, wu_ref[...])
    o_ref[...] = (g * _sigmoid(g) * u).astype(o_ref.dtype)


def ffn_up(h, w_in, layer, n_tiles, tn=512):
    k = h.shape[1]
    f = w_in.shape[2] // 2
    nj = f // tn
    return pl.pallas_call(
        _swiglu_kernel,
        grid=(nj, n_tiles),
        in_specs=[
            pl.BlockSpec((ROW_TILE, k), lambda j, i: (i, 0)),
            pl.BlockSpec((None, k, tn), lambda j, i: (layer, 0, j)),
            pl.BlockSpec((None, k, tn), lambda j, i: (layer, 0, j + nj)),
        ],
        out_specs=pl.BlockSpec((ROW_TILE, tn), lambda j, i: (i, j)),
        out_shape=jax.ShapeDtypeStruct((n_tiles * ROW_TILE, f), BF16),
        compiler_params=_params("arbitrary", "arbitrary"),
        name="ffn_up",
    )(h, w_in, w_in)


def _resid_kernel(a_ref, w_ref, x_ref, gate_ref, o_ref, *, coef):
    y = _dot(a_ref[...], w_ref[...])
    o_ref[...] = x_ref[...] + (coef * gate_ref[0]) * y


def resid_matmul(a, w, layer, x, mods, gate_idx, coef, seg_of_tile, n_tiles, tn):
    _, k, n = w.shape
    tn = min(tn, n)
    nj = n // tn
    return pl.pallas_call(
        functools.partial(_resid_kernel, coef=coef),
        grid=(nj, n_tiles),
        in_specs=[
            pl.BlockSpec((ROW_TILE, k), lambda j, i: (i, 0)),
            pl.BlockSpec((None, k, tn), lambda j, i: (layer, 0, j)),
            pl.BlockSpec((ROW_TILE, tn), lambda j, i: (i, j)),
            pl.BlockSpec((1, 1, tn), lambda j, i: (seg_of_tile(i), 0, gate_idx * nj + j)),
        ],
        out_specs=pl.BlockSpec((ROW_TILE, tn), lambda j, i: (i, j)),
        out_shape=jax.ShapeDtypeStruct((n_tiles * ROW_TILE, n), F32),
        compiler_params=_params("arbitrary", "arbitrary"),
        name="resid_matmul",
    )(a, w, x, mods)


def _proj_kernel(a_ref, w_ref, o_ref):
    o_ref[...] = _dot(a_ref[...], w_ref[...]).astype(o_ref.dtype)


def proj_plain(h, w, layer, col0, width, n_tiles, tn=512):
    k = h.shape[1]
    off = col0 // tn
    return pl.pallas_call(
        _proj_kernel,
        grid=(width // tn, n_tiles),
        in_specs=[
            pl.BlockSpec((ROW_TILE, k), lambda j, i: (i, 0)),
            pl.BlockSpec((None, k, tn), lambda j, i: (layer, 0, j + off)),
        ],
        out_specs=pl.BlockSpec((ROW_TILE, tn), lambda j, i: (i, j)),
        out_shape=jax.ShapeDtypeStruct((n_tiles * ROW_TILE, width), BF16),
        compiler_params=_params("arbitrary", "arbitrary"),
        name="proj_plain",
    )(h, w)


def _swap_pairs(y):
    lane = lax.broadcasted_iota(jnp.int32, y.shape, 1)
    nxt = pltpu.roll(y, LANES - 1, 1)
    prv = pltpu.roll(y, 1, 1)
    return jnp.where(lane % 2 == 0, nxt, prv)


def _proj_qk_kernel(a_ref, w_ref, gain_ref, *rest, head_dim, rope):
    if rope:
        cos_ref, sin_ref, o_ref = rest
    else:
        (o_ref,) = rest
    tn = w_ref.shape[1]
    for r in range(a_ref.shape[0] // QK_SUB_ROWS):
        rows = slice(r * QK_SUB_ROWS, (r + 1) * QK_SUB_ROWS)
        acc = _dot(a_ref[rows, :], w_ref[...])
        for s in range(tn // LANES):
            sl = slice(s * LANES, (s + 1) * LANES)
            y = acc[:, sl]
            sq = y * y
            if head_dim == LANES:
                ms = jnp.mean(sq, axis=-1, keepdims=True)
            else:
                lane = lax.broadcasted_iota(jnp.int32, y.shape, 1)
                low = lane < head_dim
                s_low = jnp.sum(jnp.where(low, sq, 0.0), axis=-1, keepdims=True)
                s_high = jnp.sum(jnp.where(low, 0.0, sq), axis=-1, keepdims=True)
                ms = jnp.where(low, s_low, s_high) * (1.0 / head_dim)
            y = y * lax.rsqrt(ms + EPS) * gain_ref[:, sl]
            if rope:
                y = y * cos_ref[rows, :] + _swap_pairs(y) * sin_ref[rows, :]
            o_ref[rows, sl] = y.astype(o_ref.dtype)


def proj_qk(h, w, layer, col0, width, gain_row, head_dim, rope_tabs, n_tiles, tn=512):
    k = h.shape[1]
    off = col0 // tn
    rope = rope_tabs is not None
    in_specs = [
        pl.BlockSpec((ROW_TILE, k), lambda j, i: (i, 0)),
        pl.BlockSpec((None, k, tn), lambda j, i: (layer, 0, j + off)),
        pl.BlockSpec((1, tn), lambda j, i: (0, j)),
    ]
    args = [h, w, gain_row]
    if rope:
        in_specs += [pl.BlockSpec((ROW_TILE, LANES), lambda j, i: (i, 0))] * 2
        args += list(rope_tabs)
    return pl.pallas_call(
        functools.partial(_proj_qk_kernel, head_dim=head_dim, rope=rope),
        grid=(width // tn, n_tiles),
        in_specs=in_specs,
        out_specs=pl.BlockSpec((ROW_TILE, tn), lambda j, i: (i, j)),
        out_shape=jax.ShapeDtypeStruct((n_tiles * ROW_TILE, width), BF16),
        compiler_params=_params("arbitrary", "arbitrary"),
        name="proj_qk",
    )(*args)


def _merge_kernel(oa_ref, ob_ref, oc_ref, wa_ref, wb_ref, wc_ref, ga_ref, gb_ref, gc_ref, o_ref):
    m = _sigmoid(ga_ref[...].astype(F32)) * _dot(oa_ref[...], wa_ref[...])
    m += _sigmoid(gb_ref[...].astype(F32)) * _dot(ob_ref[...], wb_ref[...])
    m += _sigmoid(gc_ref[...].astype(F32)) * _dot(oc_ref[...], wc_ref[...])
    o_ref[...] = m.astype(o_ref.dtype)


def gated_merge(oa, ob, oc, wa, wb, wc, layer, gates, n_tiles, tn=1024):
    d = wa.shape[2]
    tn = min(tn, d)
    nj = d // tn
    act = lambda o: pl.BlockSpec((ROW_TILE, o.shape[1]), lambda j, i: (i, 0))
    wgt = lambda w: pl.BlockSpec((None, w.shape[1], tn), lambda j, i: (layer, 0, j))
    gate = lambda b: pl.BlockSpec((ROW_TILE, tn), lambda j, i: (i, j + b * nj))
    return pl.pallas_call(
        _merge_kernel,
        grid=(nj, n_tiles),
        in_specs=[act(oa), act(ob), act(oc), wgt(wa), wgt(wb), wgt(wc), gate(0), gate(1), gate(2)],
        out_specs=pl.BlockSpec((ROW_TILE, tn), lambda j, i: (i, j)),
        out_shape=jax.ShapeDtypeStruct((n_tiles * ROW_TILE, d), BF16),
        compiler_params=_params("arbitrary", "arbitrary"),
        name="gated_merge",
    )(oa, ob, oc, wa, wb, wc, gates, gates, gates)


def _build_vaug(vaug_ref, vl_ref, vc_ref):
    seq = vl_ref.shape[0]
    vaug_ref[0:seq, 0:HEAD_DIM] = vl_ref[...]
    vaug_ref[seq:, 0:HEAD_DIM] = vc_ref[...]
    vaug_ref[:, HEAD_DIM:] = jnp.ones((vaug_ref.shape[0], HEAD_DIM), vaug_ref.dtype)


def _attend(q, parts):
    acc = m_run = None
    for k, vaug, bias in parts:
        s = _dot_t(q, k)
        if bias is not None:
            s = s + bias
        m_new = jnp.max(s, axis=-1, keepdims=True)
        if acc is not None:
            m_new = jnp.maximum(m_run, m_new)
        o = _dot(jnp.exp2(s - m_new).astype(vaug.dtype), vaug)
        acc = o if acc is None else jnp.exp2(m_run - m_new) * acc + o
        m_run = m_new
    return acc[:, :HEAD_DIM] / acc[:, HEAD_DIM:]


def _key_parts(kl_ref, kc_ref, vaug_ref, latent):
    seq = kl_ref.shape[0]
    parts = []
    if latent:
        for c in range(seq // KEY_CHUNK):
            sl = slice(c * KEY_CHUNK, (c + 1) * KEY_CHUNK)
            parts.append((kl_ref[sl, :], vaug_ref[sl, :], None))
    parts.append((kc_ref[...], vaug_ref[seq:, :], None))
    return parts


def _gqa_kernel(q_ref, kl_ref, kc_ref, vl_ref, vc_ref, o_ref, vaug_ref, *, group, n_lat, with_ctx):
    qt = pl.program_id(2)
    pl.when(qt == 0)(lambda: _build_vaug(vaug_ref, vl_ref, vc_ref))

    def run(latent):
        q = jnp.concatenate([q_ref[:, g * HEAD_DIM:(g + 1) * HEAD_DIM] for g in range(group)], axis=0)
        o = _attend(q, _key_parts(kl_ref, kc_ref, vaug_ref, latent))
        for g in range(group):
            o_ref[:, g * HEAD_DIM:(g + 1) * HEAD_DIM] = o[g * Q_TILE:(g + 1) * Q_TILE].astype(o_ref.dtype)

    if with_ctx:
        pl.when(qt < n_lat)(lambda: run(True))
        pl.when(qt >= n_lat)(lambda: run(False))
    else:
        run(True)


def _q_row_block(b, qt, n_lat, n_batch):
    return jnp.where(qt < n_lat, b * n_lat + qt, n_batch * n_lat + b)


def gqa_attention(q, k, v, n_batch, seq, ctx_len, group, with_ctx):
    rows = q.shape[0]
    kvh = k.shape[1] // HEAD_DIM
    n_lat = seq // Q_TILE
    ctx_blk0 = n_batch * seq // ctx_len
    gw = group * HEAD_DIM
    qmap = lambda b, h, t: (_q_row_block(b, t, n_lat, n_batch), h)
    lat = pl.BlockSpec((seq, HEAD_DIM), lambda b, h, t: (b, h))
    ctx = pl.BlockSpec((ctx_len, HEAD_DIM), lambda b, h, t: (ctx_blk0 + b, h))
    return pl.pallas_call(
        functools.partial(_gqa_kernel, group=group, n_lat=n_lat, with_ctx=with_ctx),
        grid=(n_batch, kvh, n_lat + int(with_ctx)),
        in_specs=[pl.BlockSpec((Q_TILE, gw), qmap), lat, ctx, lat, ctx],
        out_specs=pl.BlockSpec((Q_TILE, gw), qmap),
        out_shape=jax.ShapeDtypeStruct((rows, q.shape[1]), BF16),
        scratch_shapes=[pltpu.VMEM((seq + ctx_len, 2 * HEAD_DIM), BF16)],
        compiler_params=_params("arbitrary", "arbitrary", "arbitrary"),
        name="gqa_attention",
    )(q, k, k, v, v)


def _diff_kernel(q_ref, kl_ref, kc_ref, vl_ref, vc_ref, lam_ref, g_ref, o_ref, vaug_ref, *,
                 lam_init, n_lat, with_ctx):
    qt = pl.program_id(2)
    pl.when(qt == 0)(lambda: _build_vaug(vaug_ref, vl_ref, vc_ref))
    lp = lam_ref[...]
    lam = (jnp.exp(jnp.sum(lp[0:1] * lp[1:2], axis=-1, keepdims=True))
           - jnp.exp(jnp.sum(lp[2:3] * lp[3:4], axis=-1, keepdims=True)) + lam_init)

    def run(latent):
        q = q_ref[...]
        lane = lax.broadcasted_iota(jnp.int32, q.shape, 1)
        zero = jnp.zeros_like(q)
        q2 = jnp.concatenate([jnp.where(lane < C_QK_DIM, q, zero), jnp.where(lane < C_QK_DIM, zero, q)], axis=0)
        o2 = _attend(q2, _key_parts(kl_ref, kc_ref, vaug_ref, latent))
        o = o2[:Q_TILE] - lam * o2[Q_TILE:]
        o = o * lax.rsqrt(jnp.mean(o * o, axis=-1, keepdims=True) + EPS) * g_ref[...]
        o_ref[...] = (o * (1.0 - lam_init)).astype(o_ref.dtype)

    if with_ctx:
        pl.when(qt < n_lat)(lambda: run(True))
        pl.when(qt >= n_lat)(lambda: run(False))
    else:
        run(True)


def diff_attention(q, k, v, lam_params, subln, lam_init, n_batch, seq, ctx_len, with_ctx):
    rows = q.shape[0]
    n_heads = v.shape[1] // HEAD_DIM
    n_lat = seq // Q_TILE
    ctx_blk0 = n_batch * seq // ctx_len
    qmap = lambda b, h, t: (_q_row_block(b, t, n_lat, n_batch), h)
    lat = pl.BlockSpec((seq, HEAD_DIM), lambda b, h, t: (b, h))
    ctx = pl.BlockSpec((ctx_len, HEAD_DIM), lambda b, h, t: (ctx_blk0 + b, h))
    full = lambda a: pl.BlockSpec(a.shape, lambda b, h, t: (0,) * a.ndim)
    subln = subln.reshape(1, -1)
    return pl.pallas_call(
        functools.partial(_diff_kernel, lam_init=lam_init, n_lat=n_lat, with_ctx=with_ctx),
        grid=(n_batch, n_heads, n_lat + int(with_ctx)),
        in_specs=[pl.BlockSpec((Q_TILE, HEAD_DIM), qmap), lat, ctx, lat, ctx, full(lam_params), full(subln)],
        out_specs=pl.BlockSpec((Q_TILE, HEAD_DIM), qmap),
        out_shape=jax.ShapeDtypeStruct((rows, v.shape[1]), BF16),
        scratch_shapes=[pltpu.VMEM((seq + ctx_len, 2 * HEAD_DIM), BF16)],
        compiler_params=_params("arbitrary", "arbitrary", "arbitrary"),
        name="diff_attention",
    )(q, k, k, v, v, lam_params, subln)


def _na_plan(rows):
    kh = min(NA_KH, rows)
    band = min(NA_ROWS + kh, rows)
    patterns, types = [], []
    for blk in range(rows // NA_ROWS):
        r0 = blk * NA_ROWS
        bs = min(max(r0 - kh // 2, 0), rows - band)
        pat = []
        for qr in range(r0, r0 + NA_ROWS):
            rs = min(max(qr - kh // 2, 0), rows - kh)
            pat.append(tuple((kr - qr + kh - 1) if rs <= kr < rs + kh else -1 for kr in range(bs, bs + band)))
        pat = tuple(pat)
        if pat not in patterns:
            patterns.append(pat)
        types.append(patterns.index(pat))
    return band, np.asarray(types, np.int32), patterns


def _na_bias_kernel(rb_ref, o_ref, *, kh, patterns):
    h = pl.program_id(0)
    n_dc = 2 * NA_KW - 1
    base = h * ((2 * NA_KH - 1) * n_dc)
    qc = lax.broadcasted_iota(jnp.int32, (GRID_W, GRID_W), 0)
    kc = lax.broadcasted_iota(jnp.int32, (GRID_W, GRID_W), 1)
    dc = kc - qc
    cs = jnp.clip(qc - NA_KW // 2, 0, GRID_W - NA_KW)
    col_ok = (kc >= cs) & (kc < cs + NA_KW)
    neg = jnp.full((GRID_W, GRID_W), NEG_BIAS, F32)
    used = sorted({a for pat in patterns for row in pat for a in row if a >= 0})
    toep = {}
    for a in used:
        a_full = a + (NA_KH - kh)
        t = neg
        for b in range(n_dc):
            t = jnp.where(dc == b - (NA_KW - 1), rb_ref[base + a_full * n_dc + b] * LOG2E, t)
        toep[a] = jnp.where(col_ok, t, neg)
    for t_id, pat in enumerate(patterns):
        for qr, row in enumerate(pat):
            blocks = [toep[a] if a >= 0 else neg for a in row]
            o_ref[0, t_id, qr * GRID_W:(qr + 1) * GRID_W, :] = jnp.concatenate(blocks, axis=1)


def na_bias_tables(rel_bias, rows):
    n_heads = rel_bias.shape[0]
    kh = min(NA_KH, rows)
    band, _, patterns = _na_plan(rows)
    shape = (n_heads, len(patterns), Q_TILE, band * GRID_W)
    return pl.pallas_call(
        functools.partial(_na_bias_kernel, kh=kh, patterns=patterns),
        grid=(n_heads,),
        in_specs=[pl.BlockSpec(memory_space=pltpu.SMEM)],
        out_specs=pl.BlockSpec((1,) + shape[1:], lambda h: (h, 0, 0, 0)),
        out_shape=jax.ShapeDtypeStruct(shape, F32),
        compiler_params=_params("arbitrary"),
        name="na_bias_tables",
    )(rel_bias.reshape(-1))


def _na_kernel(types_ref, q_ref, kl_ref, kc_ref, vl_ref, vc_ref, bias_ref, o_ref, vaug_ref, *,
               rows, band, n_lat, with_ctx):
    del types_ref
    rt = pl.program_id(2)
    kh = min(NA_KH, rows)
    seq = kl_ref.shape[0]
    heads = [slice(j * HEAD_DIM, (j + 1) * HEAD_DIM) for j in range(NA_HEADS)]

    def build():
        for j, hs in enumerate(heads):
            vaug_ref[j, 0:seq, 0:HEAD_DIM] = vl_ref[:, hs]
            vaug_ref[j, seq:, 0:HEAD_DIM] = vc_ref[:, hs]
            vaug_ref[j, :, HEAD_DIM:] = jnp.ones((vaug_ref.shape[1], HEAD_DIM), vaug_ref.dtype)

    pl.when(rt == 0)(build)
    ctx_part = lambda j, hs: (kc_ref[:, hs], vaug_ref[j, seq:, :], None)

    def latent():
        bs = jnp.clip(rt * NA_ROWS - kh // 2, 0, rows - band)
        band_sl = pl.ds(pl.multiple_of(bs * GRID_W, GRID_W), band * GRID_W)
        for j, hs in enumerate(heads):
            parts = [(kl_ref[band_sl, hs], vaug_ref[j, band_sl, :], bias_ref[j, 0]), ctx_part(j, hs)]
            o_ref[:, hs] = _attend(q_ref[:, hs], parts).astype(o_ref.dtype)

    def context():
        for j, hs in enumerate(heads):
            o_ref[:, hs] = _attend(q_ref[:, hs], [ctx_part(j, hs)]).astype(o_ref.dtype)

    if with_ctx:
        pl.when(rt < n_lat)(latent)
        pl.when(rt >= n_lat)(context)
    else:
        latent()


def neighbourhood_attention(q, k, v, bias_tabs, n_batch, seq, ctx_len, with_ctx):
    rows_tok = q.shape[0]
    n_heads = q.shape[1] // HEAD_DIM
    rows = seq // GRID_W
    band, types, _ = _na_plan(rows)
    n_lat = seq // Q_TILE
    ctx_blk0 = n_batch * seq // ctx_len
    types = jnp.asarray(np.concatenate([types, types[-1:]]))
    hw = NA_HEADS * HEAD_DIM
    assert n_heads % NA_HEADS == 0
    qmap = lambda b, h, t, ty: (_q_row_block(b, t, n_lat, n_batch), h)
    lat = pl.BlockSpec((seq, hw), lambda b, h, t, ty: (b, h))
    ctx = pl.BlockSpec((ctx_len, hw), lambda b, h, t, ty: (ctx_blk0 + b, h))
    bias = pl.BlockSpec((NA_HEADS, 1, Q_TILE, band * GRID_W), lambda b, h, t, ty: (h, ty[t], 0, 0))
    return pl.pallas_call(
        functools.partial(_na_kernel, rows=rows, band=band, n_lat=n_lat, with_ctx=with_ctx),
        grid_spec=pltpu.PrefetchScalarGridSpec(
            num_scalar_prefetch=1,
            grid=(n_batch, n_heads // NA_HEADS, n_lat + int(with_ctx)),
            in_specs=[pl.BlockSpec((Q_TILE, hw), qmap), lat, ctx, lat, ctx, bias],
            out_specs=pl.BlockSpec((Q_TILE, hw), qmap),
            scratch_shapes=[pltpu.VMEM((NA_HEADS, seq + ctx_len, 2 * HEAD_DIM), BF16)],
        ),
        out_shape=jax.ShapeDtypeStruct((rows_tok, q.shape[1]), BF16),
        compiler_params=_params("arbitrary", "arbitrary", "arbitrary"),
        name="neighbourhood_attention",
    )(types, q, k, k, v, v, bias_tabs)


def _rope_tables(seq, dim, n_batch, n_ctx_rows):
    t = jnp.arange(seq, dtype=jnp.int32)
    row = (t // GRID_W).astype(F32)
    col = (t % GRID_W).astype(F32)
    n_pairs = dim // 4
    inv = ROPE_THETA ** (-jnp.arange(n_pairs, dtype=F32) / n_pairs)
    ang = jnp.concatenate([row[:, None] * inv, col[:, None] * inv], axis=-1)
    cos = jnp.repeat(jnp.cos(ang), 2, axis=-1)
    sin = jnp.stack([-jnp.sin(ang), jnp.sin(ang)], axis=-1).reshape(seq, dim)
    reps = LANES // dim
    cos, sin = jnp.tile(cos, (n_batch, reps)), jnp.tile(sin, (n_batch, reps))
    cos = jnp.concatenate([cos, jnp.ones((n_ctx_rows, LANES), F32)], axis=0)
    sin = jnp.concatenate([sin, jnp.zeros((n_ctx_rows, LANES), F32)], axis=0)
    return cos, sin


def kernel(x, c, ctx, c_ctx, w_ada, b_ada, norm_g, ffn1_in, ffn1_out, ffn2_in, ffn2_out, w_in,
           qk_gain_a, qk_gain_b, qk_gain_c, na_rel_bias, diff_lambda, diff_subln,
           w_br_a, w_br_b, w_br_c, w_out):
    n_batch, seq, d = x.shape
    ctx_len = ctx.shape[1]
    depth = w_ada.shape[0]
    a_q_w, b_w, c_w = w_br_a.shape[1], w_br_b.shape[1], w_br_c.shape[1]
    a_kv_w = (w_in.shape[2] - a_q_w - 3 * b_w - 3 * c_w - 3 * d) // 2
    group = a_q_w // a_kv_w
    assert ctx_len == Q_TILE and seq % ROW_TILE == 0 and (n_batch * ctx_len) % ROW_TILE == 0
    assert seq % KEY_CHUNK == 0 and n_batch + 1 <= MOD_ROWS
    n_lat_rows = n_batch * seq
    n_ctx_rows = n_batch * ctx_len
    lat_tiles = n_lat_rows // ROW_TILE
    all_tiles = lat_tiles + n_ctx_rows // ROW_TILE
    tiles_per_batch = seq // ROW_TILE
    seg_of_tile = lambda i: jnp.minimum(i // tiles_per_batch, n_batch)

    sizes = (a_q_w, a_kv_w, a_kv_w, b_w, b_w, b_w, c_w, c_w, c_w, 3 * d)
    offs = [int(o) for o in np.cumsum((0,) + sizes)]

    cvec = jnp.concatenate([c, c_ctx[None], jnp.zeros((MOD_ROWS - n_batch - 1, d), F32)], axis=0)
    mods_all = ada_modulation(cvec, w_ada, b_ada)

    rope_a = _rope_tables(seq, HEAD_DIM, n_batch, n_ctx_rows)
    rope_c = _rope_tables(seq, C_QK_DIM, n_batch, n_ctx_rows)
    rows = seq // GRID_W

    xs = jnp.concatenate([x.reshape(n_lat_rows, d), ctx.reshape(n_ctx_rows, d)], axis=0)

    w1i, w1o, w2i, w2o = (w.astype(BF16) for w in (ffn1_in, ffn1_out, ffn2_in, ffn2_out))
    wp, wa, wb, wc, wo = (w.astype(BF16) for w in (w_in, w_br_a, w_br_b, w_br_c, w_out))

    def gain_row(gain, width, scale):
        return jnp.tile(gain * scale, width // gain.shape[0]).reshape(1, width)

    sa, sc = HEAD_DIM ** -0.5 * LOG2E, C_QK_DIM ** -0.5 * LOG2E

    for l in range(depth):
        last = l == depth - 1
        with_ctx = not last
        lam_init = 0.8 - 0.6 * math.exp(-0.3 * l)
        mods = mods_all[l].reshape(MOD_ROWS, 1, N_MOD * d)

        h = norm_mod(xs, norm_g[l, 0], mods, 0, 1, seg_of_tile, all_tiles)
        a = ffn_up(h, w1i, l, all_tiles)
        xs = resid_matmul(a, w1o, l, xs, mods, 2, 0.5, seg_of_tile, all_tiles, tn=512)

        h = norm_mod(xs, norm_g[l, 1], mods, 3, 4, seg_of_tile, all_tiles)
        qa = proj_qk(h, wp, l, offs[0], a_q_w, gain_row(qk_gain_a[l, 0], a_q_w, sa), HEAD_DIM, rope_a, all_tiles)
        ka = proj_qk(h, wp, l, offs[1], a_kv_w, gain_row(qk_gain_a[l, 1], a_kv_w, 1.0), HEAD_DIM, rope_a, all_tiles)
        va = proj_plain(h, wp, l, offs[2], a_kv_w, all_tiles)
        qb = proj_qk(h, wp, l, offs[3], b_w, gain_row(qk_gain_b[l, 0], b_w, sa), HEAD_DIM, None, all_tiles)
        kb = proj_qk(h, wp, l, offs[4], b_w, gain_row(qk_gain_b[l, 1], b_w, 1.0), HEAD_DIM, None, all_tiles)
        vb = proj_plain(h, wp, l, offs[5], b_w, all_tiles)
        qc = proj_qk(h, wp, l, offs[6], c_w, gain_row(qk_gain_c[l, 0], c_w, sc), C_QK_DIM, rope_c, all_tiles)
        kc = proj_qk(h, wp, l, offs[7], c_w, gain_row(qk_gain_c[l, 1], c_w, 1.0), C_QK_DIM, rope_c, all_tiles)
        vc = proj_plain(h, wp, l, offs[8], c_w, all_tiles)
        mix_tiles = all_tiles if with_ctx else lat_tiles
        gates = proj_plain(h, wp, l, offs[9], 3 * d, mix_tiles, tn=min(1024, d))

        o_a = gqa_attention(qa, ka, va, n_batch, seq, ctx_len, group, with_ctx)
        bias_tabs = na_bias_tables(na_rel_bias[l], rows)
        o_b = neighbourhood_attention(qb, kb, vb, bias_tabs, n_batch, seq, ctx_len, with_ctx)
        o_c = diff_attention(qc, kc, vc, diff_lambda[l], diff_subln[l], lam_init, n_batch, seq, ctx_len, with_ctx)

        m = gated_merge(o_a, o_b, o_c, wa, wb, wc, l, gates, mix_tiles)
        xs = resid_matmul(m, wo, l, xs, mods, 5, 1.0, seg_of_tile, mix_tiles, tn=1024)

        h = norm_mod(xs, norm_g[l, 2], mods, 6, 7, seg_of_tile, mix_tiles)
        a = ffn_up(h, w2i, l, mix_tiles)
        xs = resid_matmul(a, w2o, l, xs, mods, 8, 0.5, seg_of_tile, mix_tiles, tn=512)

    return xs[:n_lat_rows].reshape(n_batch, seq, d)
```

```python
import functools
import math
import typing

import numpy as np
import jax
import jax.numpy as jnp
from jax import lax
from jax.experimental import pallas as pl
from jax.experimental.pallas import tpu as pltpu

F32 = jnp.float32
BF16 = jnp.bfloat16

GRID_W = 64
HEAD_DIM = 128
C_QK_DIM = 64
NA_KH = 8
NA_KW = 16
ROPE_THETA = 10000.0
EPS = 1e-6
N_MOD = 9
LANES = 128
MOD_ROWS = 8
NEG_BIAS = -1e30
LOG2E = math.log2(math.e)
KEY_CHUNK = 512

V7X_VMEM_BYTES = 64 * 1024 * 1024
VMEM_LIMIT = (V7X_VMEM_BYTES * 3) // 4

ROW_TILE = 512
BF16_SUBLANES = 16
MAX_W_CHUNKS = 8
QK_SUB_ROWS = 256
NA_HEADS = 4
Q_TILE = 256
NA_ROWS = Q_TILE // GRID_W


def _params(*sem):
    return pltpu.CompilerParams(dimension_semantics=sem, vmem_limit_bytes=VMEM_LIMIT)


def _sigmoid(v):
    return 1.0 / (1.0 + jnp.exp(-v))


def _dot(a, b):
    return jnp.dot(a, b, preferred_element_type=F32)


def _dot_t(a, b):
    return lax.dot_general(a, b, (((1,), (1,)), ((), ())), preferred_element_type=F32)


def _ada_kernel(c_ref, w_ref, b_ref, o_ref):
    c = c_ref[...]
    a = (c * _sigmoid(c)).astype(BF16)
    o_ref[0] = _dot(a, w_ref[0].astype(BF16)) + b_ref[0]


def ada_modulation(cvec, w_ada, b_ada, tn=512):
    depth, d, n = w_ada.shape
    return pl.pallas_call(
        _ada_kernel,
        grid=(depth, n // tn),
        in_specs=[
            pl.BlockSpec((MOD_ROWS, d), lambda l, j: (0, 0)),
            pl.BlockSpec((1, d, tn), lambda l, j: (l, 0, j)),
            pl.BlockSpec((1, 1, tn), lambda l, j: (l, 0, j)),
        ],
        out_specs=pl.BlockSpec((1, MOD_ROWS, tn), lambda l, j: (l, 0, j)),
        out_shape=jax.ShapeDtypeStruct((depth, MOD_ROWS, n), F32),
        compiler_params=_params("arbitrary", "arbitrary"),
        name="ada_mod",
    )(cvec, w_ada, b_ada.reshape(depth, 1, n))


def _normmod_kernel(x_ref, g_ref, shift_ref, scale_ref, o_ref):
    x = x_ref[...]
    y = x * lax.rsqrt(jnp.mean(x * x, axis=-1, keepdims=True) + EPS) * g_ref[...]
    o_ref[...] = (y * (1.0 + scale_ref[0]) + shift_ref[0]).astype(o_ref.dtype)


def norm_mod(x, g, mods, shift_idx, scale_idx, seg_of_tile, n_tiles, tr=256):
    d = x.shape[1]
    return pl.pallas_call(
        _normmod_kernel,
        grid=(n_tiles * (ROW_TILE // tr),),
        in_specs=[
            pl.BlockSpec((tr, d), lambda i: (i, 0)),
            pl.BlockSpec((1, d), lambda i: (0, 0)),
            pl.BlockSpec((1, 1, d), lambda i: (seg_of_tile(i * tr // ROW_TILE), 0, shift_idx)),
            pl.BlockSpec((1, 1, d), lambda i: (seg_of_tile(i * tr // ROW_TILE), 0, scale_idx)),
        ],
        out_specs=pl.BlockSpec((tr, d), lambda i: (i, 0)),
        out_shape=jax.ShapeDtypeStruct((n_tiles * ROW_TILE, d), BF16),
        compiler_params=_params("arbitrary"),
        name="norm_mod",
    )(x, g.reshape(1, d), mods, mods)


class _WStream(typing.NamedTuple):
    hbm: typing.Any
    layer: int
    col0: int
    wbuf: typing.Any
    stage: typing.Any
    sem: typing.Any


def _n_chunks(k, n_row_tiles):
    for nk in range(min(MAX_W_CHUNKS, n_row_tiles - 1), 0, -1):
        if k % nk == 0 and (k // nk) % BF16_SUBLANES == 0:
            return nk
    raise ValueError(f"no weight chunking for K={k} with {n_row_tiles} row tiles")


def _w_copy(st, tile, chunk, slot):
    _, ck, tn = st.stage.shape
    rows = pl.ds(pl.multiple_of(chunk * ck, BF16_SUBLANES), ck)
    cols = pl.ds(pl.multiple_of(st.col0 + tile * tn, LANES), tn)
    return pltpu.make_async_copy(st.hbm.at[st.layer, rows, cols], st.stage.at[slot], st.sem.at[slot])


def _w_round(st, buf, chunk, slot):
    ck = st.stage.shape[1]
    rows = pl.ds(pl.multiple_of(chunk * ck, BF16_SUBLANES), ck)
    st.wbuf[buf, rows, :] = st.stage[slot].astype(st.wbuf.dtype)


def _weight_pipeline(streams, nj):
    j, i = pl.program_id(0), pl.program_id(1)
    nk = streams[0].wbuf.shape[1] // streams[0].stage.shape[1]

    @pl.when((j == 0) & (i == 0))
    def _prime():
        for st in streams:
            _w_copy(st, 0, 0, 0).start()
        for c in range(nk):
            for st in streams:
                if c + 1 < nk:
                    _w_copy(st, 0, c + 1, (c + 1) % 2).start()
                _w_copy(st, 0, c, c % 2).wait()
                _w_round(st, 0, c, c % 2)

    @pl.when(j + 1 < nj)
    def _prefetch():
        @pl.when((i >= 1) & (i <= nk))
        def _():
            for st in streams:
                _w_copy(st, j + 1, i - 1, 0).wait()
                _w_round(st, (j + 1) % 2, i - 1, 0)

        @pl.when(i < nk)
        def _():
            for st in streams:
                _w_copy(st, j + 1, i, 0).start()

    return j % 2


def _w_scratch(k, tn, nk):
    return [pltpu.VMEM((2, k, tn), BF16), pltpu.VMEM((2, k // nk, tn), F32), pltpu.SemaphoreType.DMA((2,))]


_HBM = pl.BlockSpec(memory_space=pl.ANY)


def _swiglu_kernel(a_ref, w_hbm, o_ref, wg, sg, semg, wu, su, semu, *, layer, nj):
    tn = wg.shape[2]
    buf = _weight_pipeline([_WStream(w_hbm, layer, 0, wg, sg, semg),
                            _WStream(w_hbm, layer, nj * tn, wu, su, semu)], nj)
    a = a_ref[...]
    g = _dot(a, wg[buf])
    u = _dot(a, wu[buf])
    o_ref[...] = (g * _sigmoid(g) * u).astype(o_ref.dtype)


def ffn_up(h, w_in, layer, n_tiles, tn=512):
    k = h.shape[1]
    f = w_in.shape[2] // 2
    nj = f // tn
    nk = _n_chunks(k, n_tiles)
    return pl.pallas_call(
        functools.partial(_swiglu_kernel, layer=layer, nj=nj),
        grid=(nj, n_tiles),
        in_specs=[pl.BlockSpec((ROW_TILE, k), lambda j, i: (i, 0)), _HBM],
        out_specs=pl.BlockSpec((ROW_TILE, tn), lambda j, i: (i, j)),
        out_shape=jax.ShapeDtypeStruct((n_tiles * ROW_TILE, f), BF16),
        scratch_shapes=_w_scratch(k, tn, nk) + _w_scratch(k, tn, nk),
        compiler_params=_params("arbitrary", "arbitrary"),
        name="ffn_up",
    )(h, w_in)


def _resid_kernel(a_ref, w_hbm, x_ref, gate_ref, o_ref, wb, sb, semb, *, layer, nj, coef):
    buf = _weight_pipeline([_WStream(w_hbm, layer, 0, wb, sb, semb)], nj)
    y = _dot(a_ref[...], wb[buf])
    o_ref[...] = x_ref[...] + (coef * gate_ref[0]) * y


def resid_matmul(a, w, layer, x, mods, gate_idx, coef, seg_of_tile, n_tiles, tn):
    _, k, n = w.shape
    tn = min(tn, n)
    nj = n // tn
    nk = _n_chunks(k, n_tiles)
    return pl.pallas_call(
        functools.partial(_resid_kernel, layer=layer, nj=nj, coef=coef),
        grid=(nj, n_tiles),
        in_specs=[
            pl.BlockSpec((ROW_TILE, k), lambda j, i: (i, 0)),
            _HBM,
            pl.BlockSpec((ROW_TILE, tn), lambda j, i: (i, j)),
            pl.BlockSpec((1, 1, tn), lambda j, i: (seg_of_tile(i), 0, gate_idx * nj + j)),
        ],
        out_specs=pl.BlockSpec((ROW_TILE, tn), lambda j, i: (i, j)),
        out_shape=jax.ShapeDtypeStruct((n_tiles * ROW_TILE, n), F32),
        scratch_shapes=_w_scratch(k, tn, nk),
        compiler_params=_params("arbitrary", "arbitrary"),
        name="resid_matmul",
    )(a, w, x, mods)


def _proj_kernel(a_ref, w_hbm, o_ref, wb, sb, semb, *, layer, col0, nj):
    buf = _weight_pipeline([_WStream(w_hbm, layer, col0, wb, sb, semb)], nj)
    o_ref[...] = _dot(a_ref[...], wb[buf]).astype(o_ref.dtype)


def proj_plain(h, w, layer, col0, width, n_tiles, tn=512):
    k = h.shape[1]
    nj = width // tn
    nk = _n_chunks(k, n_tiles)
    return pl.pallas_call(
        functools.partial(_proj_kernel, layer=layer, col0=col0, nj=nj),
        grid=(nj, n_tiles),
        in_specs=[pl.BlockSpec((ROW_TILE, k), lambda j, i: (i, 0)), _HBM],
        out_specs=pl.BlockSpec((ROW_TILE, tn), lambda j, i: (i, j)),
        out_shape=jax.ShapeDtypeStruct((n_tiles * ROW_TILE, width), BF16),
        scratch_shapes=_w_scratch(k, tn, nk),
        compiler_params=_params("arbitrary", "arbitrary"),
        name="proj_plain",
    )(h, w)


def _swap_pairs(y):
    lane = lax.broadcasted_iota(jnp.int32, y.shape, 1)
    nxt = pltpu.roll(y, LANES - 1, 1)
    prv = pltpu.roll(y, 1, 1)
    return jnp.where(lane % 2 == 0, nxt, prv)


def _proj_qk_kernel(a_ref, w_hbm, gain_ref, *rest, layer, col0, nj, head_dim, rope):
    if rope:
        cos_ref, sin_ref, o_ref, wb, sb, semb = rest
    else:
        o_ref, wb, sb, semb = rest
    buf = _weight_pipeline([_WStream(w_hbm, layer, col0, wb, sb, semb)], nj)
    tn = wb.shape[2]
    for r in range(a_ref.shape[0] // QK_SUB_ROWS):
        rows = slice(r * QK_SUB_ROWS, (r + 1) * QK_SUB_ROWS)
        acc = _dot(a_ref[rows, :], wb[buf])
        for s in range(tn // LANES):
            sl = slice(s * LANES, (s + 1) * LANES)
            y = acc[:, sl]
            sq = y * y
            if head_dim == LANES:
                ms = jnp.mean(sq, axis=-1, keepdims=True)
            else:
                lane = lax.broadcasted_iota(jnp.int32, y.shape, 1)
                low = lane < head_dim
                s_low = jnp.sum(jnp.where(low, sq, 0.0), axis=-1, keepdims=True)
                s_high = jnp.sum(jnp.where(low, 0.0, sq), axis=-1, keepdims=True)
                ms = jnp.where(low, s_low, s_high) * (1.0 / head_dim)
            y = y * lax.rsqrt(ms + EPS) * gain_ref[:, sl]
            if rope:
                y = y * cos_ref[rows, :] + _swap_pairs(y) * sin_ref[rows, :]
            o_ref[rows, sl] = y.astype(o_ref.dtype)


def proj_qk(h, w, layer, col0, width, gain_row, head_dim, rope_tabs, n_tiles, tn=512):
    k = h.shape[1]
    nj = width // tn
    nk = _n_chunks(k, n_tiles)
    rope = rope_tabs is not None
    in_specs = [
        pl.BlockSpec((ROW_TILE, k), lambda j, i: (i, 0)),
        _HBM,
        pl.BlockSpec((1, tn), lambda j, i: (0, j)),
    ]
    args = [h, w, gain_row]
    if rope:
        in_specs += [pl.BlockSpec((ROW_TILE, LANES), lambda j, i: (i, 0))] * 2
        args += list(rope_tabs)
    return pl.pallas_call(
        functools.partial(_proj_qk_kernel, layer=layer, col0=col0, nj=nj, head_dim=head_dim, rope=rope),
        grid=(nj, n_tiles),
        in_specs=in_specs,
        out_specs=pl.BlockSpec((ROW_TILE, tn), lambda j, i: (i, j)),
        out_shape=jax.ShapeDtypeStruct((n_tiles * ROW_TILE, width), BF16),
        scratch_shapes=_w_scratch(k, tn, nk),
        compiler_params=_params("arbitrary", "arbitrary"),
        name="proj_qk",
    )(*args)


def _merge_kernel(oa_ref, ob_ref, oc_ref, wa_hbm, wb_hbm, wc_hbm, ga_ref, gb_ref, gc_ref, o_ref,
                  wa, sa, sema, wb, sb, semb, wc, sc, semc, *, layer, nj):
    buf = _weight_pipeline([_WStream(wa_hbm, layer, 0, wa, sa, sema), _WStream(wb_hbm, layer, 0, wb, sb, semb),
                            _WStream(wc_hbm, layer, 0, wc, sc, semc)], nj)
    m = _sigmoid(ga_ref[...].astype(F32)) * _dot(oa_ref[...], wa[buf])
    m += _sigmoid(gb_ref[...].astype(F32)) * _dot(ob_ref[...], wb[buf])
    m += _sigmoid(gc_ref[...].astype(F32)) * _dot(oc_ref[...], wc[buf])
    o_ref[...] = m.astype(o_ref.dtype)


def gated_merge(oa, ob, oc, wa, wb, wc, layer, gates, n_tiles, tn=1024):
    d = wa.shape[2]
    tn = min(tn, d)
    nj = d // tn
    nk = min(_n_chunks(w.shape[1], n_tiles) for w in (wa, wb, wc))
    assert all(w.shape[1] % nk == 0 and (w.shape[1] // nk) % BF16_SUBLANES == 0 for w in (wa, wb, wc))
    act = lambda o: pl.BlockSpec((ROW_TILE, o.shape[1]), lambda j, i: (i, 0))
    gate = lambda b: pl.BlockSpec((ROW_TILE, tn), lambda j, i: (i, j + b * nj))
    return pl.pallas_call(
        functools.partial(_merge_kernel, layer=layer, nj=nj),
        grid=(nj, n_tiles),
        in_specs=[act(oa), act(ob), act(oc), _HBM, _HBM, _HBM, gate(0), gate(1), gate(2)],
        out_specs=pl.BlockSpec((ROW_TILE, tn), lambda j, i: (i, j)),
        out_shape=jax.ShapeDtypeStruct((n_tiles * ROW_TILE, d), BF16),
        scratch_shapes=sum((_w_scratch(w.shape[1], tn, nk) for w in (wa, wb, wc)), []),
        compiler_params=_params("arbitrary", "arbitrary"),
        name="gated_merge",
    )(oa, ob, oc, wa, wb, wc, gates, gates, gates)


def _build_vaug(vaug_ref, vl_ref, vc_ref):
    seq = vl_ref.shape[0]
    vaug_ref[0:seq, 0:HEAD_DIM] = vl_ref[...]
    vaug_ref[seq:, 0:HEAD_DIM] = vc_ref[...]
    vaug_ref[:, HEAD_DIM:] = jnp.ones((vaug_ref.shape[0], HEAD_DIM), vaug_ref.dtype)


def _attend(q, parts):
    acc = m_run = None
    for k, vaug, bias in parts:
        s = _dot_t(q, k)
        if bias is not None:
            s = s + bias
        m_new = jnp.max(s, axis=-1, keepdims=True)
        if acc is not None:
            m_new = jnp.maximum(m_run, m_new)
        o = _dot(jnp.exp2(s - m_new).astype(vaug.dtype), vaug)
        acc = o if acc is None else jnp.exp2(m_run - m_new) * acc + o
        m_run = m_new
    return acc[:, :HEAD_DIM] / acc[:, HEAD_DIM:]


def _key_parts(kl_ref, kc_ref, vaug_ref, latent):
    seq = kl_ref.shape[0]
    parts = []
    if latent:
        for c in range(seq // KEY_CHUNK):
            sl = slice(c * KEY_CHUNK, (c + 1) * KEY_CHUNK)
            parts.append((kl_ref[sl, :], vaug_ref[sl, :], None))
    parts.append((kc_ref[...], vaug_ref[seq:, :], None))
    return parts


def _gqa_kernel(q_ref, kl_ref, kc_ref, vl_ref, vc_ref, o_ref, vaug_ref, *, group, n_lat, with_ctx):
    qt = pl.program_id(2)
    pl.when(qt == 0)(lambda: _build_vaug(vaug_ref, vl_ref, vc_ref))

    def run(latent):
        q = jnp.concatenate([q_ref[:, g * HEAD_DIM:(g + 1) * HEAD_DIM] for g in range(group)], axis=0)
        o = _attend(q, _key_parts(kl_ref, kc_ref, vaug_ref, latent))
        for g in range(group):
            o_ref[:, g * HEAD_DIM:(g + 1) * HEAD_DIM] = o[g * Q_TILE:(g + 1) * Q_TILE].astype(o_ref.dtype)

    if with_ctx:
        pl.when(qt < n_lat)(lambda: run(True))
        pl.when(qt >= n_lat)(lambda: run(False))
    else:
        run(True)


def _q_row_block(b, qt, n_lat, n_batch):
    return jnp.where(qt < n_lat, b * n_lat + qt, n_batch * n_lat + b)


def gqa_attention(q, k, v, n_batch, seq, ctx_len, group, with_ctx):
    rows = q.shape[0] if with_ctx else n_batch * seq
    kvh = k.shape[1] // HEAD_DIM
    n_lat = seq // Q_TILE
    ctx_blk0 = n_batch * seq // ctx_len
    gw = group * HEAD_DIM
    qmap = lambda b, h, t: (_q_row_block(b, t, n_lat, n_batch), h)
    lat = pl.BlockSpec((seq, HEAD_DIM), lambda b, h, t: (b, h))
    ctx = pl.BlockSpec((ctx_len, HEAD_DIM), lambda b, h, t: (ctx_blk0 + b, h))
    return pl.pallas_call(
        functools.partial(_gqa_kernel, group=group, n_lat=n_lat, with_ctx=with_ctx),
        grid=(n_batch, kvh, n_lat + int(with_ctx)),
        in_specs=[pl.BlockSpec((Q_TILE, gw), qmap), lat, ctx, lat, ctx],
        out_specs=pl.BlockSpec((Q_TILE, gw), qmap),
        out_shape=jax.ShapeDtypeStruct((rows, q.shape[1]), BF16),
        scratch_shapes=[pltpu.VMEM((seq + ctx_len, 2 * HEAD_DIM), BF16)],
        compiler_params=_params("arbitrary", "arbitrary", "arbitrary"),
        name="gqa_attention",
    )(q, k, k, v, v)


def _diff_kernel(q_ref, kl_ref, kc_ref, vl_ref, vc_ref, lam_ref, g_ref, o_ref, vaug_ref, *,
                 lam_init, n_lat, with_ctx):
    qt = pl.program_id(2)
    pl.when(qt == 0)(lambda: _build_vaug(vaug_ref, vl_ref, vc_ref))
    lp = lam_ref[...]
    lam = (jnp.exp(jnp.sum(lp[0:1] * lp[1:2], axis=-1, keepdims=True))
           - jnp.exp(jnp.sum(lp[2:3] * lp[3:4], axis=-1, keepdims=True)) + lam_init)

    def run(latent):
        q = q_ref[...]
        lane = lax.broadcasted_iota(jnp.int32, q.shape, 1)
        zero = jnp.zeros_like(q)
        q2 = jnp.concatenate([jnp.where(lane < C_QK_DIM, q, zero), jnp.where(lane < C_QK_DIM, zero, q)], axis=0)
        o2 = _attend(q2, _key_parts(kl_ref, kc_ref, vaug_ref, latent))
        o = o2[:Q_TILE] - lam * o2[Q_TILE:]
        o = o * lax.rsqrt(jnp.mean(o * o, axis=-1, keepdims=True) + EPS) * g_ref[...]
        o_ref[...] = (o * (1.0 - lam_init)).astype(o_ref.dtype)

    if with_ctx:
        pl.when(qt < n_lat)(lambda: run(True))
        pl.when(qt >= n_lat)(lambda: run(False))
    else:
        run(True)


def diff_attention(q, k, v, lam_params, subln, lam_init, n_batch, seq, ctx_len, with_ctx):
    rows = q.shape[0] if with_ctx else n_batch * seq
    n_heads = v.shape[1] // HEAD_DIM
    n_lat = seq // Q_TILE
    ctx_blk0 = n_batch * seq // ctx_len
    qmap = lambda b, h, t: (_q_row_block(b, t, n_lat, n_batch), h)
    lat = pl.BlockSpec((seq, HEAD_DIM), lambda b, h, t: (b, h))
    ctx = pl.BlockSpec((ctx_len, HEAD_DIM), lambda b, h, t: (ctx_blk0 + b, h))
    full = lambda a: pl.BlockSpec(a.shape, lambda b, h, t: (0,) * a.ndim)
    subln = subln.reshape(1, -1)
    return pl.pallas_call(
        functools.partial(_diff_kernel, lam_init=lam_init, n_lat=n_lat, with_ctx=with_ctx),
        grid=(n_batch, n_heads, n_lat + int(with_ctx)),
        in_specs=[pl.BlockSpec((Q_TILE, HEAD_DIM), qmap), lat, ctx, lat, ctx, full(lam_params), full(subln)],
        out_specs=pl.BlockSpec((Q_TILE, HEAD_DIM), qmap),
        out_shape=jax.ShapeDtypeStruct((rows, v.shape[1]), BF16),
        scratch_shapes=[pltpu.VMEM((seq + ctx_len, 2 * HEAD_DIM), BF16)],
        compiler_params=_params("arbitrary", "arbitrary", "arbitrary"),
        name="diff_attention",
    )(q, k, k, v, v, lam_params, subln)


def _na_plan(rows):
    kh = min(NA_KH, rows)
    band = min(NA_ROWS + kh, rows)
    patterns, types = [], []
    for blk in range(rows // NA_ROWS):
        r0 = blk * NA_ROWS
        bs = min(max(r0 - kh // 2, 0), rows - band)
        pat = []
        for qr in range(r0, r0 + NA_ROWS):
            rs = min(max(qr - kh // 2, 0), rows - kh)
            pat.append(tuple((kr - qr + kh - 1) if rs <= kr < rs + kh else -1 for kr in range(bs, bs + band)))
        pat = tuple(pat)
        if pat not in patterns:
            patterns.append(pat)
        types.append(patterns.index(pat))
    return band, np.asarray(types, np.int32), patterns


def _na_bias_kernel(rb_ref, o_ref, *, kh, patterns):
    h = pl.program_id(0)
    n_dc = 2 * NA_KW - 1
    base = h * ((2 * NA_KH - 1) * n_dc)
    qc = lax.broadcasted_iota(jnp.int32, (GRID_W, GRID_W), 0)
    kc = lax.broadcasted_iota(jnp.int32, (GRID_W, GRID_W), 1)
    dc = kc - qc
    cs = jnp.clip(qc - NA_KW // 2, 0, GRID_W - NA_KW)
    col_ok = (kc >= cs) & (kc < cs + NA_KW)
    neg = jnp.full((GRID_W, GRID_W), NEG_BIAS, F32)
    used = sorted({a for pat in patterns for row in pat for a in row if a >= 0})
    toep = {}
    for a in used:
        a_full = a + (NA_KH - kh)
        t = neg
        for b in range(n_dc):
            t = jnp.where(dc == b - (NA_KW - 1), rb_ref[base + a_full * n_dc + b] * LOG2E, t)
        toep[a] = jnp.where(col_ok, t, neg)
    for t_id, pat in enumerate(patterns):
        for qr, row in enumerate(pat):
            blocks = [toep[a] if a >= 0 else neg for a in row]
            o_ref[0, t_id, qr * GRID_W:(qr + 1) * GRID_W, :] = jnp.concatenate(blocks, axis=1)


def na_bias_tables(rel_bias, rows):
    n_heads = rel_bias.shape[0]
    kh = min(NA_KH, rows)
    band, _, patterns = _na_plan(rows)
    shape = (n_heads, len(patterns), Q_TILE, band * GRID_W)
    return pl.pallas_call(
        functools.partial(_na_bias_kernel, kh=kh, patterns=patterns),
        grid=(n_heads,),
        in_specs=[pl.BlockSpec(memory_space=pltpu.SMEM)],
        out_specs=pl.BlockSpec((1,) + shape[1:], lambda h: (h, 0, 0, 0)),
        out_shape=jax.ShapeDtypeStruct(shape, F32),
        compiler_params=_params("arbitrary"),
        name="na_bias_tables",
    )(rel_bias.reshape(-1))


def _na_kernel(types_ref, q_ref, kl_ref, kc_ref, vl_ref, vc_ref, bias_ref, o_ref, vaug_ref, *,
               rows, band, n_lat, with_ctx):
    del types_ref
    rt = pl.program_id(2)
    kh = min(NA_KH, rows)
    seq = kl_ref.shape[0]
    heads = [slice(j * HEAD_DIM, (j + 1) * HEAD_DIM) for j in range(NA_HEADS)]

    def build():
        for j, hs in enumerate(heads):
            vaug_ref[j, 0:seq, 0:HEAD_DIM] = vl_ref[:, hs]
            vaug_ref[j, seq:, 0:HEAD_DIM] = vc_ref[:, hs]
            vaug_ref[j, :, HEAD_DIM:] = jnp.ones((vaug_ref.shape[1], HEAD_DIM), vaug_ref.dtype)

    pl.when(rt == 0)(build)
    ctx_part = lambda j, hs: (kc_ref[:, hs], vaug_ref[j, seq:, :], None)

    def latent():
        bs = jnp.clip(rt * NA_ROWS - kh // 2, 0, rows - band)
        band_sl = pl.ds(pl.multiple_of(bs * GRID_W, GRID_W), band * GRID_W)
        for j, hs in enumerate(heads):
            parts = [(kl_ref[band_sl, hs], vaug_ref[j, band_sl, :], bias_ref[j, 0]), ctx_part(j, hs)]
            o_ref[:, hs] = _attend(q_ref[:, hs], parts).astype(o_ref.dtype)

    def context():
        for j, hs in enumerate(heads):
            o_ref[:, hs] = _attend(q_ref[:, hs], [ctx_part(j, hs)]).astype(o_ref.dtype)

    if with_ctx:
        pl.when(rt < n_lat)(latent)
        pl.when(rt >= n_lat)(context)
    else:
        latent()


def neighbourhood_attention(q, k, v, bias_tabs, n_batch, seq, ctx_len, with_ctx):
    rows_tok = q.shape[0] if with_ctx else n_batch * seq
    n_heads = q.shape[1] // HEAD_DIM
    rows = seq // GRID_W
    band, types, _ = _na_plan(rows)
    n_lat = seq // Q_TILE
    ctx_blk0 = n_batch * seq // ctx_len
    types = jnp.asarray(np.concatenate([types, types[-1:]]))
    hw = NA_HEADS * HEAD_DIM
    assert n_heads % NA_HEADS == 0
    qmap = lambda b, h, t, ty: (_q_row_block(b, t, n_lat, n_batch), h)
    lat = pl.BlockSpec((seq, hw), lambda b, h, t, ty: (b, h))
    ctx = pl.BlockSpec((ctx_len, hw), lambda b, h, t, ty: (ctx_blk0 + b, h))
    bias = pl.BlockSpec((NA_HEADS, 1, Q_TILE, band * GRID_W), lambda b, h, t, ty: (h, ty[t], 0, 0))
    return pl.pallas_call(
        functools.partial(_na_kernel, rows=rows, band=band, n_lat=n_lat, with_ctx=with_ctx),
        grid_spec=pltpu.PrefetchScalarGridSpec(
            num_scalar_prefetch=1,
            grid=(n_batch, n_heads // NA_HEADS, n_lat + int(with_ctx)),
            in_specs=[pl.BlockSpec((Q_TILE, hw), qmap), lat, ctx, lat, ctx, bias],
            out_specs=pl.BlockSpec((Q_TILE, hw), qmap),
            scratch_shapes=[pltpu.VMEM((NA_HEADS, seq + ctx_len, 2 * HEAD_DIM), BF16)],
        ),
        out_shape=jax.ShapeDtypeStruct((rows_tok, q.shape[1]), BF16),
        compiler_params=_params("arbitrary", "arbitrary", "arbitrary"),
        name="neighbourhood_attention",
    )(types, q, k, k, v, v, bias_tabs)


def _rope_tables(seq, dim, n_batch, n_ctx_rows):
    t = jnp.arange(seq, dtype=jnp.int32)
    row = (t // GRID_W).astype(F32)
    col = (t % GRID_W).astype(F32)
    n_pairs = dim // 4
    inv = ROPE_THETA ** (-jnp.arange(n_pairs, dtype=F32) / n_pairs)
    ang = jnp.concatenate([row[:, None] * inv, col[:, None] * inv], axis=-1)
    cos = jnp.repeat(jnp.cos(ang), 2, axis=-1)
    sin = jnp.stack([-jnp.sin(ang), jnp.sin(ang)], axis=-1).reshape(seq, dim)
    reps = LANES // dim
    cos, sin = jnp.tile(cos, (n_batch, reps)), jnp.tile(sin, (n_batch, reps))
    cos = jnp.concatenate([cos, jnp.ones((n_ctx_rows, LANES), F32)], axis=0)
    sin = jnp.concatenate([sin, jnp.zeros((n_ctx_rows, LANES), F32)], axis=0)
    return cos, sin


def kernel(x, c, ctx, c_ctx, w_ada, b_ada, norm_g, ffn1_in, ffn1_out, ffn2_in, ffn2_out, w_in,
           qk_gain_a, qk_gain_b, qk_gain_c, na_rel_bias, diff_lambda, diff_subln,
           w_br_a, w_br_b, w_br_c, w_out):
    n_batch, seq, d = x.shape
    ctx_len = ctx.shape[1]
    depth = w_ada.shape[0]
    a_q_w, b_w, c_w = w_br_a.shape[1], w_br_b.shape[1], w_br_c.shape[1]
    a_kv_w = (w_in.shape[2] - a_q_w - 3 * b_w - 3 * c_w - 3 * d) // 2
    group = a_q_w // a_kv_w
    assert ctx_len == Q_TILE and seq % ROW_TILE == 0 and (n_batch * ctx_len) % ROW_TILE == 0
    assert seq % KEY_CHUNK == 0 and n_batch + 1 <= MOD_ROWS
    n_lat_rows = n_batch * seq
    n_ctx_rows = n_batch * ctx_len
    lat_tiles = n_lat_rows // ROW_TILE
    all_tiles = lat_tiles + n_ctx_rows // ROW_TILE
    tiles_per_batch = seq // ROW_TILE
    seg_of_tile = lambda i: jnp.minimum(i // tiles_per_batch, n_batch)

    sizes = (a_q_w, a_kv_w, a_kv_w, b_w, b_w, b_w, c_w, c_w, c_w, 3 * d)
    offs = [int(o) for o in np.cumsum((0,) + sizes)]

    cvec = jnp.concatenate([c, c_ctx[None], jnp.zeros((MOD_ROWS - n_batch - 1, d), F32)], axis=0)
    mods_all = ada_modulation(cvec, w_ada, b_ada)

    rope_a = _rope_tables(seq, HEAD_DIM, n_batch, n_ctx_rows)
    rope_c = _rope_tables(seq, C_QK_DIM, n_batch, n_ctx_rows)
    rows = seq // GRID_W

    xs = jnp.concatenate([x.reshape(n_lat_rows, d), ctx.reshape(n_ctx_rows, d)], axis=0)

    w1i, w1o, w2i, w2o = ffn1_in, ffn1_out, ffn2_in, ffn2_out
    wp, wa, wb, wc, wo = w_in, w_br_a, w_br_b, w_br_c, w_out

    def gain_row(gain, width, scale):
        return jnp.tile(gain * scale, width // gain.shape[0]).reshape(1, width)

    sa, sc = HEAD_DIM ** -0.5 * LOG2E, C_QK_DIM ** -0.5 * LOG2E

    for l in range(depth):
        last = l == depth - 1
        with_ctx = not last
        lam_init = 0.8 - 0.6 * math.exp(-0.3 * l)
        mods = mods_all[l].reshape(MOD_ROWS, 1, N_MOD * d)

        h = norm_mod(xs, norm_g[l, 0], mods, 0, 1, seg_of_tile, all_tiles)
        a = ffn_up(h, w1i, l, all_tiles)
        xs = resid_matmul(a, w1o, l, xs, mods, 2, 0.5, seg_of_tile, all_tiles, tn=512)

        h = norm_mod(xs, norm_g[l, 1], mods, 3, 4, seg_of_tile, all_tiles)
        qa = proj_qk(h, wp, l, offs[0], a_q_w, gain_row(qk_gain_a[l, 0], a_q_w, sa), HEAD_DIM, rope_a, all_tiles)
        ka = proj_qk(h, wp, l, offs[1], a_kv_w, gain_row(qk_gain_a[l, 1], a_kv_w, 1.0), HEAD_DIM, rope_a, all_tiles)
        va = proj_plain(h, wp, l, offs[2], a_kv_w, all_tiles)
        qb = proj_qk(h, wp, l, offs[3], b_w, gain_row(qk_gain_b[l, 0], b_w, sa), HEAD_DIM, None, all_tiles)
        kb = proj_qk(h, wp, l, offs[4], b_w, gain_row(qk_gain_b[l, 1], b_w, 1.0), HEAD_DIM, None, all_tiles)
        vb = proj_plain(h, wp, l, offs[5], b_w, all_tiles)
        qc = proj_qk(h, wp, l, offs[6], c_w, gain_row(qk_gain_c[l, 0], c_w, sc), C_QK_DIM, rope_c, all_tiles)
        kc = proj_qk(h, wp, l, offs[7], c_w, gain_row(qk_gain_c[l, 1], c_w, 1.0), C_QK_DIM, rope_c, all_tiles)
        vc = proj_plain(h, wp, l, offs[8], c_w, all_tiles)
        mix_tiles = all_tiles if with_ctx else lat_tiles
        gates = proj_plain(h, wp, l, offs[9], 3 * d, mix_tiles, tn=min(1024, d))

        o_a = gqa_attention(qa, ka, va, n_batch, seq, ctx_len, group, with_ctx)
        bias_tabs = na_bias_tables(na_rel_bias[l], rows)
        o_b = neighbourhood_attention(qb, kb, vb, bias_tabs, n_batch, seq, ctx_len, with_ctx)
        o_c = diff_attention(qc, kc, vc, diff_lambda[l], diff_subln[l], lam_init, n_batch, seq, ctx_len, with_ctx)

        m = gated_merge(o_a, o_b, o_c, wa, wb, wc, l, gates, mix_tiles)
        xs = resid_matmul(m, wo, l, xs, mods, 5, 1.0, seg_of_tile, mix_tiles, tn=1024)

        h = norm_mod(xs, norm_g[l, 2], mods, 6, 7, seg_of_tile, mix_tiles)
        a = ffn_up(h, w2i, l, mix_tiles)
        xs = resid_matmul(a, w2o, l, xs, mods, 8, 0.5, seg_of_tile, mix_tiles, tn=512)

    return xs[:n_lat_rows].reshape(n_batch, seq, d)
```

```python
import functools
import math
import typing

import numpy as np
import jax
import jax.numpy as jnp
from jax import lax
from jax.experimental import pallas as pl
from jax.experimental.pallas import tpu as pltpu

F32 = jnp.float32
BF16 = jnp.bfloat16

GRID_W = 64
HEAD_DIM = 128
C_QK_DIM = 64
NA_KH = 8
NA_KW = 16
ROPE_THETA = 10000.0
EPS = 1e-6
N_MOD = 9
LANES = 128
MOD_ROWS = 8
NEG_BIAS = -1e30
LOG2E = math.log2(math.e)
KEY_CHUNK = 512

V7X_VMEM_BYTES = 64 * 1024 * 1024
VMEM_LIMIT = (V7X_VMEM_BYTES * 7) // 8

ROW_TILE = 512
BF16_SUBLANES = 16
MAX_W_CHUNKS = 8
W_STAGE_SLOTS = 2
MAX_ROW_TILE = 1088
NA_HEADS = 4
Q_TILE = 256
NA_ROWS = Q_TILE // GRID_W


def _params(*sem):
    return pltpu.CompilerParams(dimension_semantics=sem, vmem_limit_bytes=VMEM_LIMIT)


def _sigmoid(v):
    return 1.0 / (1.0 + jnp.exp(-v))


def _dot(a, b):
    return jnp.dot(a, b, preferred_element_type=F32)


def _dot_t(a, b):
    return lax.dot_general(a, b, (((1,), (1,)), ((), ())), preferred_element_type=F32)


def _ada_kernel(c_ref, w_ref, b_ref, o_ref):
    c = c_ref[...]
    a = (c * _sigmoid(c)).astype(BF16)
    o_ref[0] = _dot(a, w_ref[0].astype(BF16)) + b_ref[0]


def ada_modulation(cvec, w_ada, b_ada, tn=512):
    depth, d, n = w_ada.shape
    return pl.pallas_call(
        _ada_kernel,
        grid=(depth, n // tn),
        in_specs=[
            pl.BlockSpec((MOD_ROWS, d), lambda l, j: (0, 0)),
            pl.BlockSpec((1, d, tn), lambda l, j: (l, 0, j)),
            pl.BlockSpec((1, 1, tn), lambda l, j: (l, 0, j)),
        ],
        out_specs=pl.BlockSpec((1, MOD_ROWS, tn), lambda l, j: (l, 0, j)),
        out_shape=jax.ShapeDtypeStruct((depth, MOD_ROWS, n), F32),
        compiler_params=_params("arbitrary", "arbitrary"),
        name="ada_mod",
    )(cvec, w_ada, b_ada.reshape(depth, 1, n))


def _normmod_kernel(x_ref, g_ref, shift_ref, scale_ref, o_ref):
    x = x_ref[...]
    y = x * lax.rsqrt(jnp.mean(x * x, axis=-1, keepdims=True) + EPS) * g_ref[...]
    o_ref[...] = (y * (1.0 + scale_ref[0]) + shift_ref[0]).astype(o_ref.dtype)


def norm_mod(x, g, mods, shift_idx, scale_idx, seg_of_tile, n_tiles, tr=256):
    d = x.shape[1]
    return pl.pallas_call(
        _normmod_kernel,
        grid=(n_tiles * (ROW_TILE // tr),),
        in_specs=[
            pl.BlockSpec((tr, d), lambda i: (i, 0)),
            pl.BlockSpec((1, d), lambda i: (0, 0)),
            pl.BlockSpec((1, 1, d), lambda i: (seg_of_tile(i * tr // ROW_TILE), 0, shift_idx)),
            pl.BlockSpec((1, 1, d), lambda i: (seg_of_tile(i * tr // ROW_TILE), 0, scale_idx)),
        ],
        out_specs=pl.BlockSpec((tr, d), lambda i: (i, 0)),
        out_shape=jax.ShapeDtypeStruct((n_tiles * ROW_TILE, d), BF16),
        compiler_params=_params("arbitrary"),
        name="norm_mod",
    )(x, g.reshape(1, d), mods, mods)


class _WStream(typing.NamedTuple):
    hbm: typing.Any
    layer: int
    col0: int
    wbuf: typing.Any
    stage: typing.Any
    sem: typing.Any


def _n_chunks(k, n_row_tiles):
    for nk in range(min(MAX_W_CHUNKS, W_STAGE_SLOTS * (n_row_tiles - 1)), 0, -1):
        if k % nk == 0 and (k // nk) % BF16_SUBLANES == 0:
            return nk
    raise ValueError(f"no weight chunking for K={k} with {n_row_tiles} row tiles")


def _big_row_tile(n_rows):
    for tm in range(MAX_ROW_TILE, 0, -2 * BF16_SUBLANES):
        if n_rows % tm == 0:
            return tm
    raise ValueError(f"no row tile for {n_rows} rows")


def _w_copy(st, tile, chunk, slot):
    _, ck, tn = st.stage.shape
    rows = pl.ds(pl.multiple_of(chunk * ck, BF16_SUBLANES), ck)
    cols = pl.ds(pl.multiple_of(st.col0 + tile * tn, LANES), tn)
    return pltpu.make_async_copy(st.hbm.at[st.layer, rows, cols], st.stage.at[slot], st.sem.at[slot])


def _w_round(st, buf, chunk, slot):
    ck = st.stage.shape[1]
    rows = pl.ds(pl.multiple_of(chunk * ck, BF16_SUBLANES), ck)
    st.wbuf[buf, rows, :] = st.stage[slot].astype(st.wbuf.dtype)


def _weight_pipeline(streams, nj, ni):
    j, i = pl.program_id(0), pl.program_id(1)
    nk = streams[0].wbuf.shape[1] // streams[0].stage.shape[1]
    cps = max(1, -(-nk // max(ni - 1, 1)))
    assert cps <= W_STAGE_SLOTS and (nj == 1 or (ni - 1) * cps >= nk)

    @pl.when((j == 0) & (i == 0))
    def _prime():
        for st in streams:
            _w_copy(st, 0, 0, 0).start()
        for c in range(nk):
            for st in streams:
                if c + 1 < nk:
                    _w_copy(st, 0, c + 1, (c + 1) % W_STAGE_SLOTS).start()
                _w_copy(st, 0, c, c % W_STAGE_SLOTS).wait()
                _w_round(st, 0, c, c % W_STAGE_SLOTS)

    @pl.when(j + 1 < nj)
    def _prefetch():
        for s in range(cps):
            prev = (i - 1) * cps + s

            @pl.when((i >= 1) & (prev < nk))
            def _(prev=prev, s=s):
                for st in streams:
                    _w_copy(st, j + 1, prev, s).wait()
                    _w_round(st, (j + 1) % 2, prev, s)

        for s in range(cps):
            cur = i * cps + s

            @pl.when(cur < nk)
            def _(cur=cur, s=s):
                for st in streams:
                    _w_copy(st, j + 1, cur, s).start()

    return j % 2


def _w_scratch(k, tn, nk):
    return [pltpu.VMEM((2, k, tn), BF16), pltpu.VMEM((W_STAGE_SLOTS, k // nk, tn), F32),
            pltpu.SemaphoreType.DMA((W_STAGE_SLOTS,))]


_HBM = pl.BlockSpec(memory_space=pl.ANY)


def _row_halves(ref):
    half = ref.shape[0] // 2
    return (slice(0, half), slice(half, 2 * half))


def _swiglu_kernel(a_ref, w_hbm, o_ref, wg, sg, semg, wu, su, semu, *, layer, nj, ni):
    tn = wg.shape[2]
    buf = _weight_pipeline([_WStream(w_hbm, layer, 0, wg, sg, semg),
                            _WStream(w_hbm, layer, nj * tn, wu, su, semu)], nj, ni)
    for rows in _row_halves(a_ref):
        a = a_ref[rows, :]
        g = _dot(a, wg[buf])
        u = _dot(a, wu[buf])
        o_ref[rows, :] = (g * _sigmoid(g) * u).astype(o_ref.dtype)


def ffn_up(h, w_in, layer, n_rows, tn=512):
    k = h.shape[1]
    f = w_in.shape[2] // 2
    nj = f // tn
    tm = _big_row_tile(n_rows)
    ni = n_rows // tm
    nk = _n_chunks(k, ni)
    return pl.pallas_call(
        functools.partial(_swiglu_kernel, layer=layer, nj=nj, ni=ni),
        grid=(nj, ni),
        in_specs=[pl.BlockSpec((tm, k), lambda j, i: (i, 0)), _HBM],
        out_specs=pl.BlockSpec((tm, tn), lambda j, i: (i, j)),
        out_shape=jax.ShapeDtypeStruct((n_rows, f), BF16),
        scratch_shapes=_w_scratch(k, tn, nk) + _w_scratch(k, tn, nk),
        compiler_params=_params("arbitrary", "arbitrary"),
        name="ffn_up",
    )(h, w_in)


def _resid_kernel(a_ref, w_hbm, x_ref, gate_ref, o_ref, wb, sb, semb, *, layer, nj, ni, coef):
    buf = _weight_pipeline([_WStream(w_hbm, layer, 0, wb, sb, semb)], nj, ni)
    y = _dot(a_ref[...], wb[buf])
    o_ref[...] = x_ref[...] + (coef * gate_ref[0]) * y


def resid_matmul(a, w, layer, x, mods, gate_idx, coef, seg_of_tile, n_tiles, tn):
    _, k, n = w.shape
    tn = min(tn, n)
    nj = n // tn
    nk = _n_chunks(k, n_tiles)
    return pl.pallas_call(
        functools.partial(_resid_kernel, layer=layer, nj=nj, ni=n_tiles, coef=coef),
        grid=(nj, n_tiles),
        in_specs=[
            pl.BlockSpec((ROW_TILE, k), lambda j, i: (i, 0)),
            _HBM,
            pl.BlockSpec((ROW_TILE, tn), lambda j, i: (i, j)),
            pl.BlockSpec((1, 1, tn), lambda j, i: (seg_of_tile(i), 0, gate_idx * nj + j)),
        ],
        out_specs=pl.BlockSpec((ROW_TILE, tn), lambda j, i: (i, j)),
        out_shape=jax.ShapeDtypeStruct((n_tiles * ROW_TILE, n), F32),
        scratch_shapes=_w_scratch(k, tn, nk),
        compiler_params=_params("arbitrary", "arbitrary"),
        name="resid_matmul",
    )(a, w, x, mods)


def _proj_kernel(a_ref, w_hbm, o_ref, wb, sb, semb, *, layer, col0, nj, ni):
    buf = _weight_pipeline([_WStream(w_hbm, layer, col0, wb, sb, semb)], nj, ni)
    for rows in _row_halves(a_ref):
        o_ref[rows, :] = _dot(a_ref[rows, :], wb[buf]).astype(o_ref.dtype)


def proj_plain(h, w, layer, col0, width, n_rows, tn=512):
    k = h.shape[1]
    nj = width // tn
    tm = _big_row_tile(n_rows)
    ni = n_rows // tm
    nk = _n_chunks(k, ni)
    return pl.pallas_call(
        functools.partial(_proj_kernel, layer=layer, col0=col0, nj=nj, ni=ni),
        grid=(nj, ni),
        in_specs=[pl.BlockSpec((tm, k), lambda j, i: (i, 0)), _HBM],
        out_specs=pl.BlockSpec((tm, tn), lambda j, i: (i, j)),
        out_shape=jax.ShapeDtypeStruct((n_rows, width), BF16),
        scratch_shapes=_w_scratch(k, tn, nk),
        compiler_params=_params("arbitrary", "arbitrary"),
        name="proj_plain",
    )(h, w)


def _swap_pairs(y):
    lane = lax.broadcasted_iota(jnp.int32, y.shape, 1)
    nxt = pltpu.roll(y, LANES - 1, 1)
    prv = pltpu.roll(y, 1, 1)
    return jnp.where(lane % 2 == 0, nxt, prv)


def _proj_qk_kernel(a_ref, w_hbm, gain_ref, *rest, layer, col0, nj, ni, head_dim, rope):
    if rope:
        cos_ref, sin_ref, o_ref, wb, sb, semb = rest
    else:
        o_ref, wb, sb, semb = rest
    buf = _weight_pipeline([_WStream(w_hbm, layer, col0, wb, sb, semb)], nj, ni)
    tn = wb.shape[2]
    for rows in _row_halves(a_ref):
        acc = _dot(a_ref[rows, :], wb[buf])
        for s in range(tn // LANES):
            sl = slice(s * LANES, (s + 1) * LANES)
            y = acc[:, sl]
            sq = y * y
            if head_dim == LANES:
                ms = jnp.mean(sq, axis=-1, keepdims=True)
            else:
                lane = lax.broadcasted_iota(jnp.int32, y.shape, 1)
                low = lane < head_dim
                s_low = jnp.sum(jnp.where(low, sq, 0.0), axis=-1, keepdims=True)
                s_high = jnp.sum(jnp.where(low, 0.0, sq), axis=-1, keepdims=True)
                ms = jnp.where(low, s_low, s_high) * (1.0 / head_dim)
            y = y * lax.rsqrt(ms + EPS) * gain_ref[:, sl]
            if rope:
                y = y * cos_ref[rows, :] + _swap_pairs(y) * sin_ref[rows, :]
            o_ref[rows, sl] = y.astype(o_ref.dtype)


def proj_qk(h, w, layer, col0, width, gain_row, head_dim, rope_tabs, n_rows, tn=512):
    k = h.shape[1]
    nj = width // tn
    tm = _big_row_tile(n_rows)
    ni = n_rows // tm
    nk = _n_chunks(k, ni)
    rope = rope_tabs is not None
    in_specs = [
        pl.BlockSpec((tm, k), lambda j, i: (i, 0)),
        _HBM,
        pl.BlockSpec((1, tn), lambda j, i: (0, j)),
    ]
    args = [h, w, gain_row]
    if rope:
        in_specs += [pl.BlockSpec((tm, LANES), lambda j, i: (i, 0))] * 2
        args += list(rope_tabs)
    return pl.pallas_call(
        functools.partial(_proj_qk_kernel, layer=layer, col0=col0, nj=nj, ni=ni, head_dim=head_dim, rope=rope),
        grid=(nj, ni),
        in_specs=in_specs,
        out_specs=pl.BlockSpec((tm, tn), lambda j, i: (i, j)),
        out_shape=jax.ShapeDtypeStruct((n_rows, width), BF16),
        scratch_shapes=_w_scratch(k, tn, nk),
        compiler_params=_params("arbitrary", "arbitrary"),
        name="proj_qk",
    )(*args)


def _merge_kernel(oa_ref, ob_ref, oc_ref, wa_hbm, wb_hbm, wc_hbm, ga_ref, gb_ref, gc_ref, o_ref,
                  wa, sa, sema, wb, sb, semb, wc, sc, semc, *, layer, nj, ni):
    buf = _weight_pipeline([_WStream(wa_hbm, layer, 0, wa, sa, sema), _WStream(wb_hbm, layer, 0, wb, sb, semb),
                            _WStream(wc_hbm, layer, 0, wc, sc, semc)], nj, ni)
    m = _sigmoid(ga_ref[...].astype(F32)) * _dot(oa_ref[...], wa[buf])
    m += _sigmoid(gb_ref[...].astype(F32)) * _dot(ob_ref[...], wb[buf])
    m += _sigmoid(gc_ref[...].astype(F32)) * _dot(oc_ref[...], wc[buf])
    o_ref[...] = m.astype(o_ref.dtype)


def gated_merge(oa, ob, oc, wa, wb, wc, layer, gates, n_tiles, tn=1024):
    d = wa.shape[2]
    tn = min(tn, d)
    nj = d // tn
    nk = min(_n_chunks(w.shape[1], n_tiles) for w in (wa, wb, wc))
    assert all(w.shape[1] % nk == 0 and (w.shape[1] // nk) % BF16_SUBLANES == 0 for w in (wa, wb, wc))
    act = lambda o: pl.BlockSpec((ROW_TILE, o.shape[1]), lambda j, i: (i, 0))
    gate = lambda b: pl.BlockSpec((ROW_TILE, tn), lambda j, i: (i, j + b * nj))
    return pl.pallas_call(
        functools.partial(_merge_kernel, layer=layer, nj=nj, ni=n_tiles),
        grid=(nj, n_tiles),
        in_specs=[act(oa), act(ob), act(oc), _HBM, _HBM, _HBM, gate(0), gate(1), gate(2)],
        out_specs=pl.BlockSpec((ROW_TILE, tn), lambda j, i: (i, j)),
        out_shape=jax.ShapeDtypeStruct((n_tiles * ROW_TILE, d), BF16),
        scratch_shapes=sum((_w_scratch(w.shape[1], tn, nk) for w in (wa, wb, wc)), []),
        compiler_params=_params("arbitrary", "arbitrary"),
        name="gated_merge",
    )(oa, ob, oc, wa, wb, wc, gates, gates, gates)


def _build_vaug(vaug_ref, vl_ref, vc_ref):
    seq = vl_ref.shape[0]
    vaug_ref[0:seq, 0:HEAD_DIM] = vl_ref[...]
    vaug_ref[seq:, 0:HEAD_DIM] = vc_ref[...]
    vaug_ref[:, HEAD_DIM:] = jnp.ones((vaug_ref.shape[0], HEAD_DIM), vaug_ref.dtype)


def _attend(q, parts):
    acc = m_run = None
    for k, vaug, bias in parts:
        s = _dot_t(q, k)
        if bias is not None:
            s = s + bias
        m_new = jnp.max(s, axis=-1, keepdims=True)
        if acc is not None:
            m_new = jnp.maximum(m_run, m_new)
        o = _dot(jnp.exp2(s - m_new).astype(vaug.dtype), vaug)
        acc = o if acc is None else jnp.exp2(m_run - m_new) * acc + o
        m_run = m_new
    return acc[:, :HEAD_DIM] / acc[:, HEAD_DIM:]


def _key_parts(kl_ref, kc_ref, vaug_ref, latent):
    seq = kl_ref.shape[0]
    parts = []
    if latent:
        for c in range(seq // KEY_CHUNK):
            sl = slice(c * KEY_CHUNK, (c + 1) * KEY_CHUNK)
            parts.append((kl_ref[sl, :], vaug_ref[sl, :], None))
    parts.append((kc_ref[...], vaug_ref[seq:, :], None))
    return parts


def _gqa_kernel(q_ref, kl_ref, kc_ref, vl_ref, vc_ref, o_ref, vaug_ref, *, group, n_lat, with_ctx):
    qt = pl.program_id(2)
    pl.when(qt == 0)(lambda: _build_vaug(vaug_ref, vl_ref, vc_ref))

    def run(latent):
        q = jnp.concatenate([q_ref[:, g * HEAD_DIM:(g + 1) * HEAD_DIM] for g in range(group)], axis=0)
        o = _attend(q, _key_parts(kl_ref, kc_ref, vaug_ref, latent))
        for g in range(group):
            o_ref[:, g * HEAD_DIM:(g + 1) * HEAD_DIM] = o[g * Q_TILE:(g + 1) * Q_TILE].astype(o_ref.dtype)

    if with_ctx:
        pl.when(qt < n_lat)(lambda: run(True))
        pl.when(qt >= n_lat)(lambda: run(False))
    else:
        run(True)


def _q_row_block(b, qt, n_lat, n_batch):
    return jnp.where(qt < n_lat, b * n_lat + qt, n_batch * n_lat + b)


def gqa_attention(q, k, v, n_batch, seq, ctx_len, group, with_ctx):
    rows = q.shape[0] if with_ctx else n_batch * seq
    kvh = k.shape[1] // HEAD_DIM
    n_lat = seq // Q_TILE
    ctx_blk0 = n_batch * seq // ctx_len
    gw = group * HEAD_DIM
    qmap = lambda b, h, t: (_q_row_block(b, t, n_lat, n_batch), h)
    lat = pl.BlockSpec((seq, HEAD_DIM), lambda b, h, t: (b, h))
    ctx = pl.BlockSpec((ctx_len, HEAD_DIM), lambda b, h, t: (ctx_blk0 + b, h))
    return pl.pallas_call(
        functools.partial(_gqa_kernel, group=group, n_lat=n_lat, with_ctx=with_ctx),
        grid=(n_batch, kvh, n_lat + int(with_ctx)),
        in_specs=[pl.BlockSpec((Q_TILE, gw), qmap), lat, ctx, lat, ctx],
        out_specs=pl.BlockSpec((Q_TILE, gw), qmap),
        out_shape=jax.ShapeDtypeStruct((rows, q.shape[1]), BF16),
        scratch_shapes=[pltpu.VMEM((seq + ctx_len, 2 * HEAD_DIM), BF16)],
        compiler_params=_params("arbitrary", "arbitrary", "arbitrary"),
        name="gqa_attention",
    )(q, k, k, v, v)


def _diff_kernel(q_ref, kl_ref, kc_ref, vl_ref, vc_ref, lam_ref, g_ref, o_ref, vaug_ref, *,
                 lam_init, n_lat, with_ctx):
    qt = pl.program_id(2)
    pl.when(qt == 0)(lambda: _build_vaug(vaug_ref, vl_ref, vc_ref))
    lp = lam_ref[...]
    lam = (jnp.exp(jnp.sum(lp[0:1] * lp[1:2], axis=-1, keepdims=True))
           - jnp.exp(jnp.sum(lp[2:3] * lp[3:4], axis=-1, keepdims=True)) + lam_init)

    def run(latent):
        q = q_ref[...]
        lane = lax.broadcasted_iota(jnp.int32, q.shape, 1)
        zero = jnp.zeros_like(q)
        q2 = jnp.concatenate([jnp.where(lane < C_QK_DIM, q, zero), jnp.where(lane < C_QK_DIM, zero, q)], axis=0)
        o2 = _attend(q2, _key_parts(kl_ref, kc_ref, vaug_ref, latent))
        o = o2[:Q_TILE] - lam * o2[Q_TILE:]
        o = o * lax.rsqrt(jnp.mean(o * o, axis=-1, keepdims=True) + EPS) * g_ref[...]
        o_ref[...] = (o * (1.0 - lam_init)).astype(o_ref.dtype)

    if with_ctx:
        pl.when(qt < n_lat)(lambda: run(True))
        pl.when(qt >= n_lat)(lambda: run(False))
    else:
        run(True)


def diff_attention(q, k, v, lam_params, subln, lam_init, n_batch, seq, ctx_len, with_ctx):
    rows = q.shape[0] if with_ctx else n_batch * seq
    n_heads = v.shape[1] // HEAD_DIM
    n_lat = seq // Q_TILE
    ctx_blk0 = n_batch * seq // ctx_len
    qmap = lambda b, h, t: (_q_row_block(b, t, n_lat, n_batch), h)
    lat = pl.BlockSpec((seq, HEAD_DIM), lambda b, h, t: (b, h))
    ctx = pl.BlockSpec((ctx_len, HEAD_DIM), lambda b, h, t: (ctx_blk0 + b, h))
    full = lambda a: pl.BlockSpec(a.shape, lambda b, h, t: (0,) * a.ndim)
    subln = subln.reshape(1, -1)
    return pl.pallas_call(
        functools.partial(_diff_kernel, lam_init=lam_init, n_lat=n_lat, with_ctx=with_ctx),
        grid=(n_batch, n_heads, n_lat + int(with_ctx)),
        in_specs=[pl.BlockSpec((Q_TILE, HEAD_DIM), qmap), lat, ctx, lat, ctx, full(lam_params), full(subln)],
        out_specs=pl.BlockSpec((Q_TILE, HEAD_DIM), qmap),
        out_shape=jax.ShapeDtypeStruct((rows, v.shape[1]), BF16),
        scratch_shapes=[pltpu.VMEM((seq + ctx_len, 2 * HEAD_DIM), BF16)],
        compiler_params=_params("arbitrary", "arbitrary", "arbitrary"),
        name="diff_attention",
    )(q, k, k, v, v, lam_params, subln)


def _na_plan(rows):
    kh = min(NA_KH, rows)
    band = min(NA_ROWS + kh, rows)
    patterns, types = [], []
    for blk in range(rows // NA_ROWS):
        r0 = blk * NA_ROWS
        bs = min(max(r0 - kh // 2, 0), rows - band)
        pat = []
        for qr in range(r0, r0 + NA_ROWS):
            rs = min(max(qr - kh // 2, 0), rows - kh)
            pat.append(tuple((kr - qr + kh - 1) if rs <= kr < rs + kh else -1 for kr in range(bs, bs + band)))
        pat = tuple(pat)
        if pat not in patterns:
            patterns.append(pat)
        types.append(patterns.index(pat))
    return band, np.asarray(types, np.int32), patterns


def _na_bias_kernel(rb_ref, o_ref, *, kh, patterns):
    h = pl.program_id(0)
    n_dc = 2 * NA_KW - 1
    base = h * ((2 * NA_KH - 1) * n_dc)
    qc = lax.broadcasted_iota(jnp.int32, (GRID_W, GRID_W), 0)
    kc = lax.broadcasted_iota(jnp.int32, (GRID_W, GRID_W), 1)
    dc = kc - qc
    cs = jnp.clip(qc - NA_KW // 2, 0, GRID_W - NA_KW)
    col_ok = (kc >= cs) & (kc < cs + NA_KW)
    neg = jnp.full((GRID_W, GRID_W), NEG_BIAS, F32)
    used = sorted({a for pat in patterns for row in pat for a in row if a >= 0})
    toep = {}
    for a in used:
        a_full = a + (NA_KH - kh)
        t = neg
        for b in range(n_dc):
            t = jnp.where(dc == b - (NA_KW - 1), rb_ref[base + a_full * n_dc + b] * LOG2E, t)
        toep[a] = jnp.where(col_ok, t, neg)
    for t_id, pat in enumerate(patterns):
        for qr, row in enumerate(pat):
            blocks = [toep[a] if a >= 0 else neg for a in row]
            o_ref[0, t_id, qr * GRID_W:(qr + 1) * GRID_W, :] = jnp.concatenate(blocks, axis=1)


def na_bias_tables(rel_bias, rows):
    n_heads = rel_bias.shape[0]
    kh = min(NA_KH, rows)
    band, _, patterns = _na_plan(rows)
    shape = (n_heads, len(patterns), Q_TILE, band * GRID_W)
    return pl.pallas_call(
        functools.partial(_na_bias_kernel, kh=kh, patterns=patterns),
        grid=(n_heads,),
        in_specs=[pl.BlockSpec(memory_space=pltpu.SMEM)],
        out_specs=pl.BlockSpec((1,) + shape[1:], lambda h: (h, 0, 0, 0)),
        out_shape=jax.ShapeDtypeStruct(shape, F32),
        compiler_params=_params("arbitrary"),
        name="na_bias_tables",
    )(rel_bias.reshape(-1))


def _na_kernel(types_ref, q_ref, kl_ref, kc_ref, vl_ref, vc_ref, bias_ref, o_ref, vaug_ref, *,
               rows, band, n_lat, with_ctx):
    del types_ref
    rt = pl.program_id(2)
    kh = min(NA_KH, rows)
    seq = kl_ref.shape[0]
    heads = [slice(j * HEAD_DIM, (j + 1) * HEAD_DIM) for j in range(NA_HEADS)]

    def build():
        for j, hs in enumerate(heads):
            vaug_ref[j, 0:seq, 0:HEAD_DIM] = vl_ref[:, hs]
            vaug_ref[j, seq:, 0:HEAD_DIM] = vc_ref[:, hs]
            vaug_ref[j, :, HEAD_DIM:] = jnp.ones((vaug_ref.shape[1], HEAD_DIM), vaug_ref.dtype)

    pl.when(rt == 0)(build)
    ctx_part = lambda j, hs: (kc_ref[:, hs], vaug_ref[j, seq:, :], None)

    def latent():
        bs = jnp.clip(rt * NA_ROWS - kh // 2, 0, rows - band)
        band_sl = pl.ds(pl.multiple_of(bs * GRID_W, GRID_W), band * GRID_W)
        for j, hs in enumerate(heads):
            parts = [(kl_ref[band_sl, hs], vaug_ref[j, band_sl, :], bias_ref[j, 0]), ctx_part(j, hs)]
            o_ref[:, hs] = _attend(q_ref[:, hs], parts).astype(o_ref.dtype)

    def context():
        for j, hs in enumerate(heads):
            o_ref[:, hs] = _attend(q_ref[:, hs], [ctx_part(j, hs)]).astype(o_ref.dtype)

    if with_ctx:
        pl.when(rt < n_lat)(latent)
        pl.when(rt >= n_lat)(context)
    else:
        latent()


def neighbourhood_attention(q, k, v, bias_tabs, n_batch, seq, ctx_len, with_ctx):
    rows_tok = q.shape[0] if with_ctx else n_batch * seq
    n_heads = q.shape[1] // HEAD_DIM
    rows = seq // GRID_W
    band, types, _ = _na_plan(rows)
    n_lat = seq // Q_TILE
    ctx_blk0 = n_batch * seq // ctx_len
    types = jnp.asarray(np.concatenate([types, types[-1:]]))
    hw = NA_HEADS * HEAD_DIM
    assert n_heads % NA_HEADS == 0
    qmap = lambda b, h, t, ty: (_q_row_block(b, t, n_lat, n_batch), h)
    lat = pl.BlockSpec((seq, hw), lambda b, h, t, ty: (b, h))
    ctx = pl.BlockSpec((ctx_len, hw), lambda b, h, t, ty: (ctx_blk0 + b, h))
    bias = pl.BlockSpec((NA_HEADS, 1, Q_TILE, band * GRID_W), lambda b, h, t, ty: (h, ty[t], 0, 0))
    return pl.pallas_call(
        functools.partial(_na_kernel, rows=rows, band=band, n_lat=n_lat, with_ctx=with_ctx),
        grid_spec=pltpu.PrefetchScalarGridSpec(
            num_scalar_prefetch=1,
            grid=(n_batch, n_heads // NA_HEADS, n_lat + int(with_ctx)),
            in_specs=[pl.BlockSpec((Q_TILE, hw), qmap), lat, ctx, lat, ctx, bias],
            out_specs=pl.BlockSpec((Q_TILE, hw), qmap),
            scratch_shapes=[pltpu.VMEM((NA_HEADS, seq + ctx_len, 2 * HEAD_DIM), BF16)],
        ),
        out_shape=jax.ShapeDtypeStruct((rows_tok, q.shape[1]), BF16),
        compiler_params=_params("arbitrary", "arbitrary", "arbitrary"),
        name="neighbourhood_attention",
    )(types, q, k, k, v, v, bias_tabs)


def _rope_tables(seq, dim, n_batch, n_ctx_rows):
    t = jnp.arange(seq, dtype=jnp.int32)
    row = (t // GRID_W).astype(F32)
    col = (t % GRID_W).astype(F32)
    n_pairs = dim // 4
    inv = ROPE_THETA ** (-jnp.arange(n_pairs, dtype=F32) / n_pairs)
    ang = jnp.concatenate([row[:, None] * inv, col[:, None] * inv], axis=-1)
    cos = jnp.repeat(jnp.cos(ang), 2, axis=-1)
    sin = jnp.stack([-jnp.sin(ang), jnp.sin(ang)], axis=-1).reshape(seq, dim)
    reps = LANES // dim
    cos, sin = jnp.tile(cos, (n_batch, reps)), jnp.tile(sin, (n_batch, reps))
    cos = jnp.concatenate([cos, jnp.ones((n_ctx_rows, LANES), F32)], axis=0)
    sin = jnp.concatenate([sin, jnp.zeros((n_ctx_rows, LANES), F32)], axis=0)
    return cos, sin


def kernel(x, c, ctx, c_ctx, w_ada, b_ada, norm_g, ffn1_in, ffn1_out, ffn2_in, ffn2_out, w_in,
           qk_gain_a, qk_gain_b, qk_gain_c, na_rel_bias, diff_lambda, diff_subln,
           w_br_a, w_br_b, w_br_c, w_out):
    n_batch, seq, d = x.shape
    ctx_len = ctx.shape[1]
    depth = w_ada.shape[0]
    a_q_w, b_w, c_w = w_br_a.shape[1], w_br_b.shape[1], w_br_c.shape[1]
    a_kv_w = (w_in.shape[2] - a_q_w - 3 * b_w - 3 * c_w - 3 * d) // 2
    group = a_q_w // a_kv_w
    assert ctx_len == Q_TILE and seq % ROW_TILE == 0 and (n_batch * ctx_len) % ROW_TILE == 0
    assert seq % KEY_CHUNK == 0 and n_batch + 1 <= MOD_ROWS
    n_lat_rows = n_batch * seq
    n_ctx_rows = n_batch * ctx_len
    lat_tiles = n_lat_rows // ROW_TILE
    all_tiles = lat_tiles + n_ctx_rows // ROW_TILE
    n_all_rows = n_lat_rows + n_ctx_rows
    tiles_per_batch = seq // ROW_TILE
    seg_of_tile = lambda i: jnp.minimum(i // tiles_per_batch, n_batch)

    sizes = (a_q_w, a_kv_w, a_kv_w, b_w, b_w, b_w, c_w, c_w, c_w, 3 * d)
    offs = [int(o) for o in np.cumsum((0,) + sizes)]

    cvec = jnp.concatenate([c, c_ctx[None], jnp.zeros((MOD_ROWS - n_batch - 1, d), F32)], axis=0)
    mods_all = ada_modulation(cvec, w_ada, b_ada)

    rope_a = _rope_tables(seq, HEAD_DIM, n_batch, n_ctx_rows)
    rope_c = _rope_tables(seq, C_QK_DIM, n_batch, n_ctx_rows)
    rows = seq // GRID_W

    xs = jnp.concatenate([x.reshape(n_lat_rows, d), ctx.reshape(n_ctx_rows, d)], axis=0)

    w1i, w1o, w2i, w2o = ffn1_in, ffn1_out, ffn2_in, ffn2_out
    wp, wa, wb, wc, wo = w_in, w_br_a, w_br_b, w_br_c, w_out

    def gain_row(gain, width, scale):
        return jnp.tile(gain * scale, width // gain.shape[0]).reshape(1, width)

    sa, sc = HEAD_DIM ** -0.5 * LOG2E, C_QK_DIM ** -0.5 * LOG2E

    for l in range(depth):
        last = l == depth - 1
        with_ctx = not last
        lam_init = 0.8 - 0.6 * math.exp(-0.3 * l)
        mods = mods_all[l].reshape(MOD_ROWS, 1, N_MOD * d)

        h = norm_mod(xs, norm_g[l, 0], mods, 0, 1, seg_of_tile, all_tiles)
        a = ffn_up(h, w1i, l, n_all_rows)
        xs = resid_matmul(a, w1o, l, xs, mods, 2, 0.5, seg_of_tile, all_tiles, tn=512)

        h = norm_mod(xs, norm_g[l, 1], mods, 3, 4, seg_of_tile, all_tiles)
        qa = proj_qk(h, wp, l, offs[0], a_q_w, gain_row(qk_gain_a[l, 0], a_q_w, sa), HEAD_DIM, rope_a, n_all_rows)
        ka = proj_qk(h, wp, l, offs[1], a_kv_w, gain_row(qk_gain_a[l, 1], a_kv_w, 1.0), HEAD_DIM, rope_a, n_all_rows)
        va = proj_plain(h, wp, l, offs[2], a_kv_w, n_all_rows)
        qb = proj_qk(h, wp, l, offs[3], b_w, gain_row(qk_gain_b[l, 0], b_w, sa), HEAD_DIM, None, n_all_rows)
        kb = proj_qk(h, wp, l, offs[4], b_w, gain_row(qk_gain_b[l, 1], b_w, 1.0), HEAD_DIM, None, n_all_rows)
        vb = proj_plain(h, wp, l, offs[5], b_w, n_all_rows)
        qc = proj_qk(h, wp, l, offs[6], c_w, gain_row(qk_gain_c[l, 0], c_w, sc), C_QK_DIM, rope_c, n_all_rows)
        kc = proj_qk(h, wp, l, offs[7], c_w, gain_row(qk_gain_c[l, 1], c_w, 1.0), C_QK_DIM, rope_c, n_all_rows)
        vc = proj_plain(h, wp, l, offs[8], c_w, n_all_rows)
        mix_tiles = all_tiles if with_ctx else lat_tiles
        mix_rows = mix_tiles * ROW_TILE
        gates = proj_plain(h, wp, l, offs[9], 3 * d, mix_rows, tn=min(1024, d))

        o_a = gqa_attention(qa, ka, va, n_batch, seq, ctx_len, group, with_ctx)
        bias_tabs = na_bias_tables(na_rel_bias[l], rows)
        o_b = neighbourhood_attention(qb, kb, vb, bias_tabs, n_batch, seq, ctx_len, with_ctx)
        o_c = diff_attention(qc, kc, vc, diff_lambda[l], diff_subln[l], lam_init, n_batch, seq, ctx_len, with_ctx)

        m = gated_merge(o_a, o_b, o_c, wa, wb, wc, l, gates, mix_tiles)
        xs = resid_matmul(m, wo, l, xs, mods, 5, 1.0, seg_of_tile, mix_tiles, tn=1024)

        h = norm_mod(xs, norm_g[l, 2], mods, 6, 7, seg_of_tile, mix_tiles)
        a = ffn_up(h, w2i, l, mix_rows)
        xs = resid_matmul(a, w2o, l, xs, mods, 8, 0.5, seg_of_tile, mix_tiles, tn=512)

    return xs[:n_lat_rows].reshape(n_batch, seq, d)
```

```python
import functools
import math
import typing

import numpy as np
import jax
import jax.numpy as jnp
from jax import lax
from jax.experimental import pallas as pl
from jax.experimental.pallas import tpu as pltpu

F32 = jnp.float32
BF16 = jnp.bfloat16

GRID_W = 64
HEAD_DIM = 128
C_QK_DIM = 64
NA_KH = 8
NA_KW = 16
ROPE_THETA = 10000.0
EPS = 1e-6
N_MOD = 9
LANES = 128
MOD_ROWS = 8
NEG_BIAS = -1e30
LOG2E = math.log2(math.e)
KEY_CHUNK = 512

V7X_VMEM_BYTES = 64 * 1024 * 1024
VMEM_LIMIT = (V7X_VMEM_BYTES * 7) // 8

ROW_TILE = 512
BF16_SUBLANES = 16
MAX_W_CHUNKS = 8
W_STAGE_SLOTS = 2
QK_SUB_BLOCKS = 4
MAX_ROW_TILE = 1088
NA_HEADS = 4
Q_TILE = 256
NA_ROWS = Q_TILE // GRID_W


def _params(*sem):
    return pltpu.CompilerParams(dimension_semantics=sem, vmem_limit_bytes=VMEM_LIMIT)


def _sigmoid(v):
    return 1.0 / (1.0 + jnp.exp(-v))


def _dot(a, b):
    return jnp.dot(a, b, preferred_element_type=F32)


def _dot_t(a, b):
    return lax.dot_general(a, b, (((1,), (1,)), ((), ())), preferred_element_type=F32)


def _ada_kernel(c_ref, w_ref, b_ref, o_ref):
    c = c_ref[...]
    a = (c * _sigmoid(c)).astype(BF16)
    o_ref[0] = _dot(a, w_ref[0].astype(BF16)) + b_ref[0]


def ada_modulation(cvec, w_ada, b_ada, tn=512):
    depth, d, n = w_ada.shape
    return pl.pallas_call(
        _ada_kernel,
        grid=(depth, n // tn),
        in_specs=[
            pl.BlockSpec((MOD_ROWS, d), lambda l, j: (0, 0)),
            pl.BlockSpec((1, d, tn), lambda l, j: (l, 0, j)),
            pl.BlockSpec((1, 1, tn), lambda l, j: (l, 0, j)),
        ],
        out_specs=pl.BlockSpec((1, MOD_ROWS, tn), lambda l, j: (l, 0, j)),
        out_shape=jax.ShapeDtypeStruct((depth, MOD_ROWS, n), F32),
        compiler_params=_params("arbitrary", "arbitrary"),
        name="ada_mod",
    )(cvec, w_ada, b_ada.reshape(depth, 1, n))


def _row_sources(x, n_tiles, block_cols, col_of):
    if not isinstance(x, tuple):
        return [x], [pl.BlockSpec((ROW_TILE, block_cols), lambda *g: (g[-1], col_of(*g)))], None
    n_first = x[0].shape[0] // ROW_TILE
    assert x[0].shape[0] % ROW_TILE == 0 and x[1].shape[0] == (n_tiles - n_first) * ROW_TILE
    specs = [pl.BlockSpec((ROW_TILE, block_cols), lambda *g: (jnp.minimum(g[-1], n_first - 1), col_of(*g))),
             pl.BlockSpec((ROW_TILE, block_cols), lambda *g: (jnp.maximum(g[-1] - n_first, 0), col_of(*g)))]
    return list(x), specs, n_first


def _pick_rows(x_refs, n_first, i):
    if n_first is None:
        return x_refs[0][...]
    return jnp.where(i < n_first, x_refs[0][...], x_refs[1][...])


def _normmod_kernel(*refs, n_first):
    *x_refs, g_ref, shift_ref, scale_ref, o_ref = refs
    x = _pick_rows(x_refs, n_first, pl.program_id(0))
    y = x * lax.rsqrt(jnp.mean(x * x, axis=-1, keepdims=True) + EPS) * g_ref[...]
    o_ref[...] = (y * (1.0 + scale_ref[0]) + shift_ref[0]).astype(o_ref.dtype)


def norm_mod(x, g, mods, shift_idx, scale_idx, seg_of_tile, n_tiles):
    d = g.shape[0]
    xs, x_specs, n_first = _row_sources(x, n_tiles, d, lambda i: 0)
    return pl.pallas_call(
        functools.partial(_normmod_kernel, n_first=n_first),
        grid=(n_tiles,),
        in_specs=x_specs + [
            pl.BlockSpec((1, d), lambda i: (0, 0)),
            pl.BlockSpec((1, 1, d), lambda i: (seg_of_tile(i), 0, shift_idx)),
            pl.BlockSpec((1, 1, d), lambda i: (seg_of_tile(i), 0, scale_idx)),
        ],
        out_specs=pl.BlockSpec((ROW_TILE, d), lambda i: (i, 0)),
        out_shape=jax.ShapeDtypeStruct((n_tiles * ROW_TILE, d), BF16),
        compiler_params=_params("arbitrary"),
        name="norm_mod",
    )(*xs, g.reshape(1, d), mods, mods)


class _WStream(typing.NamedTuple):
    hbm: typing.Any
    layer: int
    col0: int
    wbuf: typing.Any
    stage: typing.Any
    sem: typing.Any


def _n_chunks(k, n_row_tiles):
    for nk in range(min(MAX_W_CHUNKS, W_STAGE_SLOTS * (n_row_tiles - 1)), 0, -1):
        if k % nk == 0 and (k // nk) % BF16_SUBLANES == 0:
            return nk
    raise ValueError(f"no weight chunking for K={k} with {n_row_tiles} row tiles")


def _big_row_tile(n_rows):
    for tm in range(MAX_ROW_TILE, 0, -QK_SUB_BLOCKS * BF16_SUBLANES):
        if n_rows % tm == 0:
            return tm
    raise ValueError(f"no row tile for {n_rows} rows")


def _w_copy(st, tile, chunk, slot):
    _, ck, tn = st.stage.shape
    rows = pl.ds(pl.multiple_of(chunk * ck, BF16_SUBLANES), ck)
    cols = pl.ds(pl.multiple_of(st.col0 + tile * tn, LANES), tn)
    return pltpu.make_async_copy(st.hbm.at[st.layer, rows, cols], st.stage.at[slot], st.sem.at[slot])


def _w_round(st, buf, chunk, slot):
    ck = st.stage.shape[1]
    rows = pl.ds(pl.multiple_of(chunk * ck, BF16_SUBLANES), ck)
    st.wbuf[buf, rows, :] = st.stage[slot].astype(st.wbuf.dtype)


def _weight_pipeline(streams, nj, ni):
    j, i = pl.program_id(0), pl.program_id(1)
    nk = streams[0].wbuf.shape[1] // streams[0].stage.shape[1]
    cps = max(1, -(-nk // max(ni - 1, 1)))
    assert cps <= W_STAGE_SLOTS and (nj == 1 or (ni - 1) * cps >= nk)

    @pl.when((j == 0) & (i == 0))
    def _prime():
        for st in streams:
            _w_copy(st, 0, 0, 0).start()
        for c in range(nk):
            for st in streams:
                if c + 1 < nk:
                    _w_copy(st, 0, c + 1, (c + 1) % W_STAGE_SLOTS).start()
                _w_copy(st, 0, c, c % W_STAGE_SLOTS).wait()
                _w_round(st, 0, c, c % W_STAGE_SLOTS)

    @pl.when(j + 1 < nj)
    def _prefetch():
        for s in range(cps):
            prev = (i - 1) * cps + s

            @pl.when((i >= 1) & (prev < nk))
            def _(prev=prev, s=s):
                for st in streams:
                    _w_copy(st, j + 1, prev, s).wait()
                    _w_round(st, (j + 1) % 2, prev, s)

        for s in range(cps):
            cur = i * cps + s

            @pl.when(cur < nk)
            def _(cur=cur, s=s):
                for st in streams:
                    _w_copy(st, j + 1, cur, s).start()

    return j % 2


def _w_scratch(k, tn, nk):
    return [pltpu.VMEM((2, k, tn), BF16), pltpu.VMEM((W_STAGE_SLOTS, k // nk, tn), F32),
            pltpu.SemaphoreType.DMA((W_STAGE_SLOTS,))]


_HBM = pl.BlockSpec(memory_space=pl.ANY)


def _row_halves(ref, n=2):
    sub = ref.shape[0] // n
    assert sub * n == ref.shape[0] and sub % BF16_SUBLANES == 0
    return tuple(slice(r * sub, (r + 1) * sub) for r in range(n))


def _swiglu_kernel(a_ref, w_hbm, o_ref, wg, sg, semg, wu, su, semu, *, layer, nj, ni):
    tn = wg.shape[2]
    buf = _weight_pipeline([_WStream(w_hbm, layer, 0, wg, sg, semg),
                            _WStream(w_hbm, layer, nj * tn, wu, su, semu)], nj, ni)
    for rows in _row_halves(a_ref):
        a = a_ref[rows, :]
        g = _dot(a, wg[buf])
        u = _dot(a, wu[buf])
        o_ref[rows, :] = (g * _sigmoid(g) * u).astype(o_ref.dtype)


def ffn_up(h, w_in, layer, n_rows, tn=512):
    k = h.shape[1]
    f = w_in.shape[2] // 2
    nj = f // tn
    tm = _big_row_tile(n_rows)
    ni = n_rows // tm
    nk = _n_chunks(k, ni)
    return pl.pallas_call(
        functools.partial(_swiglu_kernel, layer=layer, nj=nj, ni=ni),
        grid=(nj, ni),
        in_specs=[pl.BlockSpec((tm, k), lambda j, i: (i, 0)), _HBM],
        out_specs=pl.BlockSpec((tm, tn), lambda j, i: (i, j)),
        out_shape=jax.ShapeDtypeStruct((n_rows, f), BF16),
        scratch_shapes=_w_scratch(k, tn, nk) + _w_scratch(k, tn, nk),
        compiler_params=_params("arbitrary", "arbitrary"),
        name="ffn_up",
    )(h, w_in)


def _resid_kernel(a_ref, w_hbm, *refs, layer, nj, ni, coef, n_first):
    *x_refs, gate_ref, o_ref, wb, sb, semb = refs
    buf = _weight_pipeline([_WStream(w_hbm, layer, 0, wb, sb, semb)], nj, ni)
    y = _dot(a_ref[...], wb[buf])
    o_ref[...] = _pick_rows(x_refs, n_first, pl.program_id(1)) + (coef * gate_ref[0]) * y


def resid_matmul(a, w, layer, x, mods, gate_idx, coef, seg_of_tile, n_tiles, tn):
    _, k, n = w.shape
    tn = min(tn, n)
    nj = n // tn
    nk = _n_chunks(k, n_tiles)
    xs, x_specs, n_first = _row_sources(x, n_tiles, tn, lambda j, i: j)
    return pl.pallas_call(
        functools.partial(_resid_kernel, layer=layer, nj=nj, ni=n_tiles, coef=coef, n_first=n_first),
        grid=(nj, n_tiles),
        in_specs=[pl.BlockSpec((ROW_TILE, k), lambda j, i: (i, 0)), _HBM] + x_specs + [
            pl.BlockSpec((1, 1, tn), lambda j, i: (seg_of_tile(i), 0, gate_idx * nj + j)),
        ],
        out_specs=pl.BlockSpec((ROW_TILE, tn), lambda j, i: (i, j)),
        out_shape=jax.ShapeDtypeStruct((n_tiles * ROW_TILE, n), F32),
        scratch_shapes=_w_scratch(k, tn, nk),
        compiler_params=_params("arbitrary", "arbitrary"),
        name="resid_matmul",
    )(a, w, *xs, mods)


def _proj_kernel(a_ref, w_hbm, o_ref, wb, sb, semb, *, layer, col0, nj, ni):
    buf = _weight_pipeline([_WStream(w_hbm, layer, col0, wb, sb, semb)], nj, ni)
    for rows in _row_halves(a_ref):
        o_ref[rows, :] = _dot(a_ref[rows, :], wb[buf]).astype(o_ref.dtype)


def proj_plain(h, w, layer, col0, width, n_rows, tn=512):
    k = h.shape[1]
    nj = width // tn
    tm = _big_row_tile(n_rows)
    ni = n_rows // tm
    nk = _n_chunks(k, ni)
    return pl.pallas_call(
        functools.partial(_proj_kernel, layer=layer, col0=col0, nj=nj, ni=ni),
        grid=(nj, ni),
        in_specs=[pl.BlockSpec((tm, k), lambda j, i: (i, 0)), _HBM],
        out_specs=pl.BlockSpec((tm, tn), lambda j, i: (i, j)),
        out_shape=jax.ShapeDtypeStruct((n_rows, width), BF16),
        scratch_shapes=_w_scratch(k, tn, nk),
        compiler_params=_params("arbitrary", "arbitrary"),
        name="proj_plain",
    )(h, w)


def _swap_pairs(y):
    lane = lax.broadcasted_iota(jnp.int32, y.shape, 1)
    nxt = pltpu.roll(y, LANES - 1, 1)
    prv = pltpu.roll(y, 1, 1)
    return jnp.where(lane % 2 == 0, nxt, prv)


def _proj_qk_kernel(a_ref, w_hbm, gain_ref, *rest, layer, col0, nj, ni, head_dim, rope):
    if rope:
        cos_ref, sin_ref, o_ref, wb, sb, semb = rest
    else:
        o_ref, wb, sb, semb = rest
    buf = _weight_pipeline([_WStream(w_hbm, layer, col0, wb, sb, semb)], nj, ni)
    tn = wb.shape[2]
    for rows in _row_halves(a_ref, QK_SUB_BLOCKS):
        acc = _dot(a_ref[rows, :], wb[buf])
        for s in range(tn // LANES):
            sl = slice(s * LANES, (s + 1) * LANES)
            y = acc[:, sl]
            sq = y * y
            if head_dim == LANES:
                ms = jnp.mean(sq, axis=-1, keepdims=True)
            else:
                lane = lax.broadcasted_iota(jnp.int32, y.shape, 1)
                low = lane < head_dim
                s_low = jnp.sum(jnp.where(low, sq, 0.0), axis=-1, keepdims=True)
                s_high = jnp.sum(jnp.where(low, 0.0, sq), axis=-1, keepdims=True)
                ms = jnp.where(low, s_low, s_high) * (1.0 / head_dim)
            y = y * lax.rsqrt(ms + EPS) * gain_ref[:, sl]
            if rope:
                y = y * cos_ref[rows, :] + _swap_pairs(y) * sin_ref[rows, :]
            o_ref[rows, sl] = y.astype(o_ref.dtype)


def proj_qk(h, w, layer, col0, width, gain_row, head_dim, rope_tabs, n_rows, tn=512):
    k = h.shape[1]
    nj = width // tn
    tm = _big_row_tile(n_rows)
    ni = n_rows // tm
    nk = _n_chunks(k, ni)
    rope = rope_tabs is not None
    in_specs = [
        pl.BlockSpec((tm, k), lambda j, i: (i, 0)),
        _HBM,
        pl.BlockSpec((1, tn), lambda j, i: (0, j)),
    ]
    args = [h, w, gain_row]
    if rope:
        in_specs += [pl.BlockSpec((tm, LANES), lambda j, i: (i, 0))] * 2
        args += list(rope_tabs)
    return pl.pallas_call(
        functools.partial(_proj_qk_kernel, layer=layer, col0=col0, nj=nj, ni=ni, head_dim=head_dim, rope=rope),
        grid=(nj, ni),
        in_specs=in_specs,
        out_specs=pl.BlockSpec((tm, tn), lambda j, i: (i, j)),
        out_shape=jax.ShapeDtypeStruct((n_rows, width), BF16),
        scratch_shapes=_w_scratch(k, tn, nk),
        compiler_params=_params("arbitrary", "arbitrary"),
        name="proj_qk",
    )(*args)


def _merge_kernel(oa_ref, ob_ref, oc_ref, wa_hbm, wb_hbm, wc_hbm, ga_ref, gb_ref, gc_ref, o_ref,
                  wa, sa, sema, wb, sb, semb, wc, sc, semc, *, layer, nj, ni):
    buf = _weight_pipeline([_WStream(wa_hbm, layer, 0, wa, sa, sema), _WStream(wb_hbm, layer, 0, wb, sb, semb),
                            _WStream(wc_hbm, layer, 0, wc, sc, semc)], nj, ni)
    m = _sigmoid(ga_ref[...].astype(F32)) * _dot(oa_ref[...], wa[buf])
    m += _sigmoid(gb_ref[...].astype(F32)) * _dot(ob_ref[...], wb[buf])
    m += _sigmoid(gc_ref[...].astype(F32)) * _dot(oc_ref[...], wc[buf])
    o_ref[...] = m.astype(o_ref.dtype)


def gated_merge(oa, ob, oc, wa, wb, wc, layer, gates, n_tiles, tn=1024):
    d = wa.shape[2]
    tn = min(tn, d)
    nj = d // tn
    nk = min(_n_chunks(w.shape[1], n_tiles) for w in (wa, wb, wc))
    assert all(w.shape[1] % nk == 0 and (w.shape[1] // nk) % BF16_SUBLANES == 0 for w in (wa, wb, wc))
    act = lambda o: pl.BlockSpec((ROW_TILE, o.shape[1]), lambda j, i: (i, 0))
    gate = lambda b: pl.BlockSpec((ROW_TILE, tn), lambda j, i: (i, j + b * nj))
    return pl.pallas_call(
        functools.partial(_merge_kernel, layer=layer, nj=nj, ni=n_tiles),
        grid=(nj, n_tiles),
        in_specs=[act(oa), act(ob), act(oc), _HBM, _HBM, _HBM, gate(0), gate(1), gate(2)],
        out_specs=pl.BlockSpec((ROW_TILE, tn), lambda j, i: (i, j)),
        out_shape=jax.ShapeDtypeStruct((n_tiles * ROW_TILE, d), BF16),
        scratch_shapes=sum((_w_scratch(w.shape[1], tn, nk) for w in (wa, wb, wc)), []),
        compiler_params=_params("arbitrary", "arbitrary"),
        name="gated_merge",
    )(oa, ob, oc, wa, wb, wc, gates, gates, gates)


def _build_vaug(vaug_ref, vl_ref, vc_ref):
    seq = vl_ref.shape[0]
    vaug_ref[0:seq, 0:HEAD_DIM] = vl_ref[...]
    vaug_ref[seq:, 0:HEAD_DIM] = vc_ref[...]
    vaug_ref[:, HEAD_DIM:] = jnp.ones((vaug_ref.shape[0], HEAD_DIM), vaug_ref.dtype)


def _attend(q, parts):
    acc = m_run = None
    for k, vaug, bias in parts:
        s = _dot_t(q, k)
        if bias is not None:
            s = s + bias
        m_new = jnp.max(s, axis=-1, keepdims=True)
        if acc is not None:
            m_new = jnp.maximum(m_run, m_new)
        o = _dot(jnp.exp2(s - m_new).astype(vaug.dtype), vaug)
        acc = o if acc is None else jnp.exp2(m_run - m_new) * acc + o
        m_run = m_new
    return acc[:, :HEAD_DIM] / acc[:, HEAD_DIM:]


def _key_parts(kl_ref, kc_ref, vaug_ref, latent):
    seq = kl_ref.shape[0]
    parts = []
    if latent:
        for c in range(seq // KEY_CHUNK):
            sl = slice(c * KEY_CHUNK, (c + 1) * KEY_CHUNK)
            parts.append((kl_ref[sl, :], vaug_ref[sl, :], None))
    parts.append((kc_ref[...], vaug_ref[seq:, :], None))
    return parts


def _gqa_kernel(q_ref, kl_ref, kc_ref, vl_ref, vc_ref, o_ref, vaug_ref, *, group, n_lat, with_ctx):
    qt = pl.program_id(2)
    pl.when(qt == 0)(lambda: _build_vaug(vaug_ref, vl_ref, vc_ref))

    def run(latent):
        q = jnp.concatenate([q_ref[:, g * HEAD_DIM:(g + 1) * HEAD_DIM] for g in range(group)], axis=0)
        o = _attend(q, _key_parts(kl_ref, kc_ref, vaug_ref, latent))
        for g in range(group):
            o_ref[:, g * HEAD_DIM:(g + 1) * HEAD_DIM] = o[g * Q_TILE:(g + 1) * Q_TILE].astype(o_ref.dtype)

    if with_ctx:
        pl.when(qt < n_lat)(lambda: run(True))
        pl.when(qt >= n_lat)(lambda: run(False))
    else:
        run(True)


def _q_row_block(b, qt, n_lat, n_batch):
    return jnp.where(qt < n_lat, b * n_lat + qt, n_batch * n_lat + b)


def gqa_attention(q, k, v, n_batch, seq, ctx_len, group, with_ctx):
    rows = q.shape[0] if with_ctx else n_batch * seq
    kvh = k.shape[1] // HEAD_DIM
    n_lat = seq // Q_TILE
    ctx_blk0 = n_batch * seq // ctx_len
    gw = group * HEAD_DIM
    qmap = lambda b, h, t: (_q_row_block(b, t, n_lat, n_batch), h)
    lat = pl.BlockSpec((seq, HEAD_DIM), lambda b, h, t: (b, h))
    ctx = pl.BlockSpec((ctx_len, HEAD_DIM), lambda b, h, t: (ctx_blk0 + b, h))
    return pl.pallas_call(
        functools.partial(_gqa_kernel, group=group, n_lat=n_lat, with_ctx=with_ctx),
        grid=(n_batch, kvh, n_lat + int(with_ctx)),
        in_specs=[pl.BlockSpec((Q_TILE, gw), qmap), lat, ctx, lat, ctx],
        out_specs=pl.BlockSpec((Q_TILE, gw), qmap),
        out_shape=jax.ShapeDtypeStruct((rows, q.shape[1]), BF16),
        scratch_shapes=[pltpu.VMEM((seq + ctx_len, 2 * HEAD_DIM), BF16)],
        compiler_params=_params("arbitrary", "arbitrary", "arbitrary"),
        name="gqa_attention",
    )(q, k, k, v, v)


def _diff_kernel(q_ref, kl_ref, kc_ref, vl_ref, vc_ref, lam_ref, g_ref, o_ref, vaug_ref, *,
                 lam_init, n_lat, with_ctx):
    qt = pl.program_id(2)
    pl.when(qt == 0)(lambda: _build_vaug(vaug_ref, vl_ref, vc_ref))
    lp = lam_ref[...]
    lam = (jnp.exp(jnp.sum(lp[0:1] * lp[1:2], axis=-1, keepdims=True))
           - jnp.exp(jnp.sum(lp[2:3] * lp[3:4], axis=-1, keepdims=True)) + lam_init)

    def run(latent):
        q = q_ref[...]
        lane = lax.broadcasted_iota(jnp.int32, q.shape, 1)
        zero = jnp.zeros_like(q)
        q2 = jnp.concatenate([jnp.where(lane < C_QK_DIM, q, zero), jnp.where(lane < C_QK_DIM, zero, q)], axis=0)
        o2 = _attend(q2, _key_parts(kl_ref, kc_ref, vaug_ref, latent))
        o = o2[:Q_TILE] - lam * o2[Q_TILE:]
        o = o * lax.rsqrt(jnp.mean(o * o, axis=-1, keepdims=True) + EPS) * g_ref[...]
        o_ref[...] = (o * (1.0 - lam_init)).astype(o_ref.dtype)

    if with_ctx:
        pl.when(qt < n_lat)(lambda: run(True))
        pl.when(qt >= n_lat)(lambda: run(False))
    else:
        run(True)


def diff_attention(q, k, v, lam_params, subln, lam_init, n_batch, seq, ctx_len, with_ctx):
    rows = q.shape[0] if with_ctx else n_batch * seq
    n_heads = v.shape[1] // HEAD_DIM
    n_lat = seq // Q_TILE
    ctx_blk0 = n_batch * seq // ctx_len
    qmap = lambda b, h, t: (_q_row_block(b, t, n_lat, n_batch), h)
    lat = pl.BlockSpec((seq, HEAD_DIM), lambda b, h, t: (b, h))
    ctx = pl.BlockSpec((ctx_len, HEAD_DIM), lambda b, h, t: (ctx_blk0 + b, h))
    full = lambda a: pl.BlockSpec(a.shape, lambda b, h, t: (0,) * a.ndim)
    subln = subln.reshape(1, -1)
    return pl.pallas_call(
        functools.partial(_diff_kernel, lam_init=lam_init, n_lat=n_lat, with_ctx=with_ctx),
        grid=(n_batch, n_heads, n_lat + int(with_ctx)),
        in_specs=[pl.BlockSpec((Q_TILE, HEAD_DIM), qmap), lat, ctx, lat, ctx, full(lam_params), full(subln)],
        out_specs=pl.BlockSpec((Q_TILE, HEAD_DIM), qmap),
        out_shape=jax.ShapeDtypeStruct((rows, v.shape[1]), BF16),
        scratch_shapes=[pltpu.VMEM((seq + ctx_len, 2 * HEAD_DIM), BF16)],
        compiler_params=_params("arbitrary", "arbitrary", "arbitrary"),
        name="diff_attention",
    )(q, k, k, v, v, lam_params, subln)


def _na_plan(rows):
    kh = min(NA_KH, rows)
    band = min(NA_ROWS + kh, rows)
    patterns, types = [], []
    for blk in range(rows // NA_ROWS):
        r0 = blk * NA_ROWS
        bs = min(max(r0 - kh // 2, 0), rows - band)
        pat = []
        for qr in range(r0, r0 + NA_ROWS):
            rs = min(max(qr - kh // 2, 0), rows - kh)
            pat.append(tuple((kr - qr + kh - 1) if rs <= kr < rs + kh else -1 for kr in range(bs, bs + band)))
        pat = tuple(pat)
        if pat not in patterns:
            patterns.append(pat)
        types.append(patterns.index(pat))
    return band, np.asarray(types, np.int32), patterns


def _na_bias_kernel(rb_ref, o_ref, *, kh, patterns):
    h = pl.program_id(0)
    n_dc = 2 * NA_KW - 1
    base = h * ((2 * NA_KH - 1) * n_dc)
    qc = lax.broadcasted_iota(jnp.int32, (GRID_W, GRID_W), 0)
    kc = lax.broadcasted_iota(jnp.int32, (GRID_W, GRID_W), 1)
    dc = kc - qc
    cs = jnp.clip(qc - NA_KW // 2, 0, GRID_W - NA_KW)
    col_ok = (kc >= cs) & (kc < cs + NA_KW)
    neg = jnp.full((GRID_W, GRID_W), NEG_BIAS, F32)
    used = sorted({a for pat in patterns for row in pat for a in row if a >= 0})
    toep = {}
    for a in used:
        a_full = a + (NA_KH - kh)
        t = neg
        for b in range(n_dc):
            t = jnp.where(dc == b - (NA_KW - 1), rb_ref[base + a_full * n_dc + b] * LOG2E, t)
        toep[a] = jnp.where(col_ok, t, neg)
    for t_id, pat in enumerate(patterns):
        for qr, row in enumerate(pat):
            blocks = [toep[a] if a >= 0 else neg for a in row]
            o_ref[0, t_id, qr * GRID_W:(qr + 1) * GRID_W, :] = jnp.concatenate(blocks, axis=1)


def na_bias_tables(rel_bias, rows):
    n_heads = rel_bias.shape[0]
    kh = min(NA_KH, rows)
    band, _, patterns = _na_plan(rows)
    shape = (n_heads, len(patterns), Q_TILE, band * GRID_W)
    return pl.pallas_call(
        functools.partial(_na_bias_kernel, kh=kh, patterns=patterns),
        grid=(n_heads,),
        in_specs=[pl.BlockSpec(memory_space=pltpu.SMEM)],
        out_specs=pl.BlockSpec((1,) + shape[1:], lambda h: (h, 0, 0, 0)),
        out_shape=jax.ShapeDtypeStruct(shape, F32),
        compiler_params=_params("arbitrary"),
        name="na_bias_tables",
    )(rel_bias.reshape(-1))


def _na_kernel(types_ref, q_ref, kl_ref, kc_ref, vl_ref, vc_ref, bias_ref, o_ref, vaug_ref, *,
               rows, band, n_lat, with_ctx):
    del types_ref
    rt = pl.program_id(2)
    kh = min(NA_KH, rows)
    seq = kl_ref.shape[0]
    heads = [slice(j * HEAD_DIM, (j + 1) * HEAD_DIM) for j in range(NA_HEADS)]

    def build():
        for j, hs in enumerate(heads):
            vaug_ref[j, 0:seq, 0:HEAD_DIM] = vl_ref[:, hs]
            vaug_ref[j, seq:, 0:HEAD_DIM] = vc_ref[:, hs]
            vaug_ref[j, :, HEAD_DIM:] = jnp.ones((vaug_ref.shape[1], HEAD_DIM), vaug_ref.dtype)

    pl.when(rt == 0)(build)
    ctx_part = lambda j, hs: (kc_ref[:, hs], vaug_ref[j, seq:, :], None)

    def latent():
        bs = jnp.clip(rt * NA_ROWS - kh // 2, 0, rows - band)
        band_sl = pl.ds(pl.multiple_of(bs * GRID_W, GRID_W), band * GRID_W)
        for j, hs in enumerate(heads):
            parts = [(kl_ref[band_sl, hs], vaug_ref[j, band_sl, :], bias_ref[j, 0]), ctx_part(j, hs)]
            o_ref[:, hs] = _attend(q_ref[:, hs], parts).astype(o_ref.dtype)

    def context():
        for j, hs in enumerate(heads):
            o_ref[:, hs] = _attend(q_ref[:, hs], [ctx_part(j, hs)]).astype(o_ref.dtype)

    if with_ctx:
        pl.when(rt < n_lat)(latent)
        pl.when(rt >= n_lat)(context)
    else:
        latent()


def neighbourhood_attention(q, k, v, bias_tabs, n_batch, seq, ctx_len, with_ctx):
    rows_tok = q.shape[0] if with_ctx else n_batch * seq
    n_heads = q.shape[1] // HEAD_DIM
    rows = seq // GRID_W
    band, types, _ = _na_plan(rows)
    n_lat = seq // Q_TILE
    ctx_blk0 = n_batch * seq // ctx_len
    types = jnp.asarray(np.concatenate([types, types[-1:]]))
    hw = NA_HEADS * HEAD_DIM
    assert n_heads % NA_HEADS == 0
    qmap = lambda b, h, t, ty: (_q_row_block(b, t, n_lat, n_batch), h)
    lat = pl.BlockSpec((seq, hw), lambda b, h, t, ty: (b, h))
    ctx = pl.BlockSpec((ctx_len, hw), lambda b, h, t, ty: (ctx_blk0 + b, h))
    bias = pl.BlockSpec((NA_HEADS, 1, Q_TILE, band * GRID_W), lambda b, h, t, ty: (h, ty[t], 0, 0))
    return pl.pallas_call(
        functools.partial(_na_kernel, rows=rows, band=band, n_lat=n_lat, with_ctx=with_ctx),
        grid_spec=pltpu.PrefetchScalarGridSpec(
            num_scalar_prefetch=1,
            grid=(n_batch, n_heads // NA_HEADS, n_lat + int(with_ctx)),
            in_specs=[pl.BlockSpec((Q_TILE, hw), qmap), lat, ctx, lat, ctx, bias],
            out_specs=pl.BlockSpec((Q_TILE, hw), qmap),
            scratch_shapes=[pltpu.VMEM((NA_HEADS, seq + ctx_len, 2 * HEAD_DIM), BF16)],
        ),
        out_shape=jax.ShapeDtypeStruct((rows_tok, q.shape[1]), BF16),
        compiler_params=_params("arbitrary", "arbitrary", "arbitrary"),
        name="neighbourhood_attention",
    )(types, q, k, k, v, v, bias_tabs)


def _rope_tables(seq, dim, n_batch, n_ctx_rows):
    t = jnp.arange(seq, dtype=jnp.int32)
    row = (t // GRID_W).astype(F32)
    col = (t % GRID_W).astype(F32)
    n_pairs = dim // 4
    inv = ROPE_THETA ** (-jnp.arange(n_pairs, dtype=F32) / n_pairs)
    ang = jnp.concatenate([row[:, None] * inv, col[:, None] * inv], axis=-1)
    cos = jnp.repeat(jnp.cos(ang), 2, axis=-1)
    sin = jnp.stack([-jnp.sin(ang), jnp.sin(ang)], axis=-1).reshape(seq, dim)
    reps = LANES // dim
    cos, sin = jnp.tile(cos, (n_batch, reps)), jnp.tile(sin, (n_batch, reps))
    cos = jnp.concatenate([cos, jnp.ones((n_ctx_rows, LANES), F32)], axis=0)
    sin = jnp.concatenate([sin, jnp.zeros((n_ctx_rows, LANES), F32)], axis=0)
    return cos, sin


def kernel(x, c, ctx, c_ctx, w_ada, b_ada, norm_g, ffn1_in, ffn1_out, ffn2_in, ffn2_out, w_in,
           qk_gain_a, qk_gain_b, qk_gain_c, na_rel_bias, diff_lambda, diff_subln,
           w_br_a, w_br_b, w_br_c, w_out):
    n_batch, seq, d = x.shape
    ctx_len = ctx.shape[1]
    depth = w_ada.shape[0]
    a_q_w, b_w, c_w = w_br_a.shape[1], w_br_b.shape[1], w_br_c.shape[1]
    a_kv_w = (w_in.shape[2] - a_q_w - 3 * b_w - 3 * c_w - 3 * d) // 2
    group = a_q_w // a_kv_w
    assert ctx_len == Q_TILE and seq % ROW_TILE == 0 and (n_batch * ctx_len) % ROW_TILE == 0
    assert seq % KEY_CHUNK == 0 and n_batch + 1 <= MOD_ROWS
    n_lat_rows = n_batch * seq
    n_ctx_rows = n_batch * ctx_len
    lat_tiles = n_lat_rows // ROW_TILE
    all_tiles = lat_tiles + n_ctx_rows // ROW_TILE
    n_all_rows = n_lat_rows + n_ctx_rows
    tiles_per_batch = seq // ROW_TILE
    seg_of_tile = lambda i: jnp.minimum(i // tiles_per_batch, n_batch)

    sizes = (a_q_w, a_kv_w, a_kv_w, b_w, b_w, b_w, c_w, c_w, c_w, 3 * d)
    offs = [int(o) for o in np.cumsum((0,) + sizes)]

    cvec = jnp.concatenate([c, c_ctx[None], jnp.zeros((MOD_ROWS - n_batch - 1, d), F32)], axis=0)
    mods_all = ada_modulation(cvec, w_ada, b_ada)

    rope_a = _rope_tables(seq, HEAD_DIM, n_batch, n_ctx_rows)
    rope_c = _rope_tables(seq, C_QK_DIM, n_batch, n_ctx_rows)
    rows = seq // GRID_W

    xs = (x.reshape(n_lat_rows, d), ctx.reshape(n_ctx_rows, d))

    w1i, w1o, w2i, w2o = ffn1_in, ffn1_out, ffn2_in, ffn2_out
    wp, wa, wb, wc, wo = w_in, w_br_a, w_br_b, w_br_c, w_out

    def gain_row(gain, width, scale):
        return jnp.tile(gain * scale, width // gain.shape[0]).reshape(1, width)

    sa, sc = HEAD_DIM ** -0.5 * LOG2E, C_QK_DIM ** -0.5 * LOG2E

    for l in range(depth):
        last = l == depth - 1
        with_ctx = not last
        lam_init = 0.8 - 0.6 * math.exp(-0.3 * l)
        mods = mods_all[l].reshape(MOD_ROWS, 1, N_MOD * d)

        h = norm_mod(xs, norm_g[l, 0], mods, 0, 1, seg_of_tile, all_tiles)
        a = ffn_up(h, w1i, l, n_all_rows)
        xs = resid_matmul(a, w1o, l, xs, mods, 2, 0.5, seg_of_tile, all_tiles, tn=512)

        h = norm_mod(xs, norm_g[l, 1], mods, 3, 4, seg_of_tile, all_tiles)
        qa = proj_qk(h, wp, l, offs[0], a_q_w, gain_row(qk_gain_a[l, 0], a_q_w, sa), HEAD_DIM, rope_a, n_all_rows)
        ka = proj_qk(h, wp, l, offs[1], a_kv_w, gain_row(qk_gain_a[l, 1], a_kv_w, 1.0), HEAD_DIM, rope_a, n_all_rows)
        va = proj_plain(h, wp, l, offs[2], a_kv_w, n_all_rows)
        qb = proj_qk(h, wp, l, offs[3], b_w, gain_row(qk_gain_b[l, 0], b_w, sa), HEAD_DIM, None, n_all_rows)
        kb = proj_qk(h, wp, l, offs[4], b_w, gain_row(qk_gain_b[l, 1], b_w, 1.0), HEAD_DIM, None, n_all_rows)
        vb = proj_plain(h, wp, l, offs[5], b_w, n_all_rows)
        qc = proj_qk(h, wp, l, offs[6], c_w, gain_row(qk_gain_c[l, 0], c_w, sc), C_QK_DIM, rope_c, n_all_rows)
        kc = proj_qk(h, wp, l, offs[7], c_w, gain_row(qk_gain_c[l, 1], c_w, 1.0), C_QK_DIM, rope_c, n_all_rows)
        vc = proj_plain(h, wp, l, offs[8], c_w, n_all_rows)
        mix_tiles = all_tiles if with_ctx else lat_tiles
        mix_rows = mix_tiles * ROW_TILE
        gates = proj_plain(h, wp, l, offs[9], 3 * d, mix_rows, tn=min(1024, d))

        o_a = gqa_attention(qa, ka, va, n_batch, seq, ctx_len, group, with_ctx)
        bias_tabs = na_bias_tables(na_rel_bias[l], rows)
        o_b = neighbourhood_attention(qb, kb, vb, bias_tabs, n_batch, seq, ctx_len, with_ctx)
        o_c = diff_attention(qc, kc, vc, diff_lambda[l], diff_subln[l], lam_init, n_batch, seq, ctx_len, with_ctx)

        m = gated_merge(o_a, o_b, o_c, wa, wb, wc, l, gates, mix_tiles)
        xs = resid_matmul(m, wo, l, xs, mods, 5, 1.0, seg_of_tile, mix_tiles, tn=1024)

        h = norm_mod(xs, norm_g[l, 2], mods, 6, 7, seg_of_tile, mix_tiles)
        a = ffn_up(h, w2i, l, mix_rows)
        xs = resid_matmul(a, w2o, l, xs, mods, 8, 0.5, seg_of_tile, mix_tiles, tn=512)

    return xs.reshape(n_batch, seq, d)
```

```python
import functools
import math
import typing

import numpy as np
import jax
import jax.numpy as jnp
from jax import lax
from jax.experimental import pallas as pl
from jax.experimental.pallas import tpu as pltpu

F32 = jnp.float32
BF16 = jnp.bfloat16

GRID_W = 64
HEAD_DIM = 128
C_QK_DIM = 64
NA_KH = 8
NA_KW = 16
ROPE_THETA = 10000.0
EPS = 1e-6
N_MOD = 9
LANES = 128
MOD_ROWS = 8
NEG_BIAS = -1e30
LOG2E = math.log2(math.e)
KEY_CHUNK = 512

V7X_VMEM_BYTES = 64 * 1024 * 1024
VMEM_LIMIT = (V7X_VMEM_BYTES * 7) // 8

ROW_TILE = 512
BF16_SUBLANES = 16
MAX_W_CHUNKS = 8
W_STAGE_SLOTS = 2
FFN_DOWN_TN = 256
QK_SUB_BLOCKS = 4
MAX_ROW_TILE = 1088
NA_HEADS = 4
DIFF_HEADS = 4
GQA_KV_HEADS = 4
Q_TILE = 256
NA_ROWS = Q_TILE // GRID_W


def _params(*sem):
    return pltpu.CompilerParams(dimension_semantics=sem, vmem_limit_bytes=VMEM_LIMIT)


def _sigmoid(v):
    return 1.0 / (1.0 + jnp.exp(-v))


def _dot(a, b):
    return jnp.dot(a, b, preferred_element_type=F32)


def _dot_t(a, b):
    return lax.dot_general(a, b, (((1,), (1,)), ((), ())), preferred_element_type=F32)


def _ada_kernel(c_ref, w_ref, b_ref, o_ref):
    c = c_ref[...]
    a = (c * _sigmoid(c)).astype(BF16)
    o_ref[0] = _dot(a, w_ref[0].astype(BF16)) + b_ref[0]


def ada_modulation(cvec, w_ada, b_ada, tn=512):
    depth, d, n = w_ada.shape
    return pl.pallas_call(
        _ada_kernel,
        grid=(depth, n // tn),
        in_specs=[
            pl.BlockSpec((MOD_ROWS, d), lambda l, j: (0, 0)),
            pl.BlockSpec((1, d, tn), lambda l, j: (l, 0, j)),
            pl.BlockSpec((1, 1, tn), lambda l, j: (l, 0, j)),
        ],
        out_specs=pl.BlockSpec((1, MOD_ROWS, tn), lambda l, j: (l, 0, j)),
        out_shape=jax.ShapeDtypeStruct((depth, MOD_ROWS, n), F32),
        compiler_params=_params("arbitrary", "arbitrary"),
        name="ada_mod",
    )(cvec, w_ada, b_ada.reshape(depth, 1, n))


def _row_sources(x, n_tiles, block_cols, col_of):
    if not isinstance(x, tuple):
        return [x], [pl.BlockSpec((ROW_TILE, block_cols), lambda *g: (g[-1], col_of(*g)))], None
    n_first = x[0].shape[0] // ROW_TILE
    assert x[0].shape[0] % ROW_TILE == 0 and x[1].shape[0] == (n_tiles - n_first) * ROW_TILE
    specs = [pl.BlockSpec((ROW_TILE, block_cols), lambda *g: (jnp.minimum(g[-1], n_first - 1), col_of(*g))),
             pl.BlockSpec((ROW_TILE, block_cols), lambda *g: (jnp.maximum(g[-1] - n_first, 0), col_of(*g)))]
    return list(x), specs, n_first


def _pick_rows(x_refs, n_first, i):
    if n_first is None:
        return x_refs[0][...]
    return jnp.where(i < n_first, x_refs[0][...], x_refs[1][...])


def _normmod_kernel(*refs, n_first):
    *x_refs, g_ref, shift_ref, scale_ref, o_ref = refs
    x = _pick_rows(x_refs, n_first, pl.program_id(0))
    y = x * lax.rsqrt(jnp.mean(x * x, axis=-1, keepdims=True) + EPS) * g_ref[...]
    o_ref[...] = (y * (1.0 + scale_ref[0]) + shift_ref[0]).astype(o_ref.dtype)


def norm_mod(x, g, mods, shift_idx, scale_idx, seg_of_tile, n_tiles):
    d = g.shape[0]
    xs, x_specs, n_first = _row_sources(x, n_tiles, d, lambda i: 0)
    return pl.pallas_call(
        functools.partial(_normmod_kernel, n_first=n_first),
        grid=(n_tiles,),
        in_specs=x_specs + [
            pl.BlockSpec((1, d), lambda i: (0, 0)),
            pl.BlockSpec((1, 1, d), lambda i: (seg_of_tile(i), 0, shift_idx)),
            pl.BlockSpec((1, 1, d), lambda i: (seg_of_tile(i), 0, scale_idx)),
        ],
        out_specs=pl.BlockSpec((ROW_TILE, d), lambda i: (i, 0)),
        out_shape=jax.ShapeDtypeStruct((n_tiles * ROW_TILE, d), BF16),
        compiler_params=_params("arbitrary"),
        name="norm_mod",
    )(*xs, g.reshape(1, d), mods, mods)


class _WStream(typing.NamedTuple):
    hbm: typing.Any
    layer: int
    col0: int
    wbuf: typing.Any
    stage: typing.Any
    sem: typing.Any


def _n_chunks(k, n_row_tiles):
    for nk in range(min(MAX_W_CHUNKS, W_STAGE_SLOTS * (n_row_tiles - 1)), 0, -1):
        if k % nk == 0 and (k // nk) % BF16_SUBLANES == 0:
            return nk
    raise ValueError(f"no weight chunking for K={k} with {n_row_tiles} row tiles")


def _big_row_tile(n_rows):
    for tm in range(MAX_ROW_TILE, 0, -QK_SUB_BLOCKS * BF16_SUBLANES):
        if n_rows % tm == 0:
            return tm
    raise ValueError(f"no row tile for {n_rows} rows")


def _w_copy(st, tile, chunk, slot):
    _, ck, tn = st.stage.shape
    rows = pl.ds(pl.multiple_of(chunk * ck, BF16_SUBLANES), ck)
    cols = pl.ds(pl.multiple_of(st.col0 + tile * tn, LANES), tn)
    return pltpu.make_async_copy(st.hbm.at[st.layer, rows, cols], st.stage.at[slot], st.sem.at[slot])


def _w_round(st, buf, chunk, slot):
    ck = st.stage.shape[1]
    rows = pl.ds(pl.multiple_of(chunk * ck, BF16_SUBLANES), ck)
    st.wbuf[buf, rows, :] = st.stage[slot].astype(st.wbuf.dtype)


def _weight_pipeline(streams, nj, ni):
    j, i = pl.program_id(0), pl.program_id(1)
    nk = streams[0].wbuf.shape[1] // streams[0].stage.shape[1]
    cps = max(1, -(-nk // max(ni - 1, 1)))
    assert cps <= W_STAGE_SLOTS and (nj == 1 or (ni - 1) * cps >= nk)

    @pl.when((j == 0) & (i == 0))
    def _prime():
        for st in streams:
            _w_copy(st, 0, 0, 0).start()
        for c in range(nk):
            for st in streams:
                if c + 1 < nk:
                    _w_copy(st, 0, c + 1, (c + 1) % W_STAGE_SLOTS).start()
                _w_copy(st, 0, c, c % W_STAGE_SLOTS).wait()
                _w_round(st, 0, c, c % W_STAGE_SLOTS)

    @pl.when(j + 1 < nj)
    def _prefetch():
        for s in range(cps):
            prev = (i - 1) * cps + s

            @pl.when((i >= 1) & (prev < nk))
            def _(prev=prev, s=s):
                for st in streams:
                    _w_copy(st, j + 1, prev, s).wait()
                    _w_round(st, (j + 1) % 2, prev, s)

        for s in range(cps):
            cur = i * cps + s

            @pl.when(cur < nk)
            def _(cur=cur, s=s):
                for st in streams:
                    _w_copy(st, j + 1, cur, s).start()

    return j % 2


def _w_scratch(k, tn, nk):
    return [pltpu.VMEM((2, k, tn), BF16), pltpu.VMEM((W_STAGE_SLOTS, k // nk, tn), F32),
            pltpu.SemaphoreType.DMA((W_STAGE_SLOTS,))]


_HBM = pl.BlockSpec(memory_space=pl.ANY)


def _row_halves(ref, n=2):
    sub = ref.shape[0] // n
    assert sub * n == ref.shape[0] and sub % BF16_SUBLANES == 0
    return tuple(slice(r * sub, (r + 1) * sub) for r in range(n))


def _swiglu_kernel(a_ref, w_hbm, o_ref, wg, sg, semg, wu, su, semu, *, layer, nj, ni):
    tn = wg.shape[2]
    buf = _weight_pipeline([_WStream(w_hbm, layer, 0, wg, sg, semg),
                            _WStream(w_hbm, layer, nj * tn, wu, su, semu)], nj, ni)
    for rows in _row_halves(a_ref):
        a = a_ref[rows, :]
        g = _dot(a, wg[buf])
        u = _dot(a, wu[buf])
        o_ref[rows, :] = (g * _sigmoid(g) * u).astype(o_ref.dtype)


def ffn_up(h, w_in, layer, n_rows, tn=512):
    k = h.shape[1]
    f = w_in.shape[2] // 2
    nj = f // tn
    tm = _big_row_tile(n_rows)
    ni = n_rows // tm
    nk = _n_chunks(k, ni)
    return pl.pallas_call(
        functools.partial(_swiglu_kernel, layer=layer, nj=nj, ni=ni),
        grid=(nj, ni),
        in_specs=[pl.BlockSpec((tm, k), lambda j, i: (i, 0)), _HBM],
        out_specs=pl.BlockSpec((tm, tn), lambda j, i: (i, j)),
        out_shape=jax.ShapeDtypeStruct((n_rows, f), BF16),
        scratch_shapes=_w_scratch(k, tn, nk) + _w_scratch(k, tn, nk),
        compiler_params=_params("arbitrary", "arbitrary"),
        name="ffn_up",
    )(h, w_in)


def _resid_kernel(a_ref, w_hbm, *refs, layer, nj, ni, coef, n_first, seq, n_batch):
    *x_refs, gate_lo_ref, gate_hi_ref, o_ref, wb, sb, semb = refs
    buf = _weight_pipeline([_WStream(w_hbm, layer, 0, wb, sb, semb)], nj, ni)
    i = pl.program_id(1)
    tm = o_ref.shape[0]
    hi_start = jnp.minimum((i * tm + tm - 1) // seq, n_batch) * seq
    x = _pick_rows(x_refs, n_first, i)
    for rows in _row_halves(o_ref):
        row_id = i * tm + rows.start + lax.broadcasted_iota(jnp.int32, (rows.stop - rows.start, 1), 0)
        gate = jnp.where(row_id >= hi_start, gate_hi_ref[0], gate_lo_ref[0])
        o_ref[rows, :] = x[rows, :] + (coef * gate) * _dot(a_ref[rows, :], wb[buf])


def resid_matmul(a, w, layer, x, mods, gate_idx, coef, seq, n_batch, n_rows, tn):
    _, k, n = w.shape
    tn = min(tn, n)
    nj = n // tn
    tm = ROW_TILE if isinstance(x, tuple) else _big_row_tile(n_rows)
    ni = n_rows // tm
    nk = _n_chunks(k, ni)
    if isinstance(x, tuple):
        xs, x_specs, n_first = _row_sources(x, ni, tn, lambda j, i: j)
    else:
        xs, x_specs, n_first = [x], [pl.BlockSpec((tm, tn), lambda j, i: (i, j))], None
    seg = lambda r: jnp.minimum(r // seq, n_batch)
    gate_spec = lambda row_of: pl.BlockSpec((1, 1, tn), lambda j, i: (seg(row_of(i)), 0, gate_idx * nj + j))
    return pl.pallas_call(
        functools.partial(_resid_kernel, layer=layer, nj=nj, ni=ni, coef=coef, n_first=n_first,
                          seq=seq, n_batch=n_batch),
        grid=(nj, ni),
        in_specs=[pl.BlockSpec((tm, k), lambda j, i: (i, 0)), _HBM] + x_specs + [
            gate_spec(lambda i: i * tm), gate_spec(lambda i: i * tm + tm - 1)],
        out_specs=pl.BlockSpec((tm, tn), lambda j, i: (i, j)),
        out_shape=jax.ShapeDtypeStruct((n_rows, n), F32),
        scratch_shapes=_w_scratch(k, tn, nk),
        compiler_params=_params("arbitrary", "arbitrary"),
        name="resid_matmul",
    )(a, w, *xs, mods, mods)


def _proj_kernel(a_ref, w_hbm, o_ref, wb, sb, semb, *, layer, col0, nj, ni):
    buf = _weight_pipeline([_WStream(w_hbm, layer, col0, wb, sb, semb)], nj, ni)
    for rows in _row_halves(a_ref):
        o_ref[rows, :] = _dot(a_ref[rows, :], wb[buf]).astype(o_ref.dtype)


def proj_plain(h, w, layer, col0, width, n_rows, tn=512):
    k = h.shape[1]
    nj = width // tn
    tm = _big_row_tile(n_rows)
    ni = n_rows // tm
    nk = _n_chunks(k, ni)
    return pl.pallas_call(
        functools.partial(_proj_kernel, layer=layer, col0=col0, nj=nj, ni=ni),
        grid=(nj, ni),
        in_specs=[pl.BlockSpec((tm, k), lambda j, i: (i, 0)), _HBM],
        out_specs=pl.BlockSpec((tm, tn), lambda j, i: (i, j)),
        out_shape=jax.ShapeDtypeStruct((n_rows, width), BF16),
        scratch_shapes=_w_scratch(k, tn, nk),
        compiler_params=_params("arbitrary", "arbitrary"),
        name="proj_plain",
    )(h, w)


def _swap_pairs(y):
    lane = lax.broadcasted_iota(jnp.int32, y.shape, 1)
    nxt = pltpu.roll(y, LANES - 1, 1)
    prv = pltpu.roll(y, 1, 1)
    return jnp.where(lane % 2 == 0, nxt, prv)


def _proj_qk_kernel(a_ref, w_hbm, gain_ref, *rest, layer, col0, nj, ni, head_dim, rope):
    if rope:
        cos_ref, sin_ref, o_ref, wb, sb, semb = rest
    else:
        o_ref, wb, sb, semb = rest
    buf = _weight_pipeline([_WStream(w_hbm, layer, col0, wb, sb, semb)], nj, ni)
    tn = wb.shape[2]
    for rows in _row_halves(a_ref, QK_SUB_BLOCKS):
        acc = _dot(a_ref[rows, :], wb[buf])
        for s in range(tn // LANES):
            sl = slice(s * LANES, (s + 1) * LANES)
            y = acc[:, sl]
            sq = y * y
            if head_dim == LANES:
                ms = jnp.mean(sq, axis=-1, keepdims=True)
            else:
                lane = lax.broadcasted_iota(jnp.int32, y.shape, 1)
                low = lane < head_dim
                s_low = jnp.sum(jnp.where(low, sq, 0.0), axis=-1, keepdims=True)
                s_high = jnp.sum(jnp.where(low, 0.0, sq), axis=-1, keepdims=True)
                ms = jnp.where(low, s_low, s_high) * (1.0 / head_dim)
            y = y * lax.rsqrt(ms + EPS) * gain_ref[:, sl]
            if rope:
                y = y * cos_ref[rows, :] + _swap_pairs(y) * sin_ref[rows, :]
            o_ref[rows, sl] = y.astype(o_ref.dtype)


def proj_qk(h, w, layer, col0, width, gain_row, head_dim, rope_tabs, n_rows, tn=512):
    k = h.shape[1]
    nj = width // tn
    tm = _big_row_tile(n_rows)
    ni = n_rows // tm
    nk = _n_chunks(k, ni)
    rope = rope_tabs is not None
    in_specs = [
        pl.BlockSpec((tm, k), lambda j, i: (i, 0)),
        _HBM,
        pl.BlockSpec((1, tn), lambda j, i: (0, j)),
    ]
    args = [h, w, gain_row]
    if rope:
        in_specs += [pl.BlockSpec((tm, LANES), lambda j, i: (i, 0))] * 2
        args += list(rope_tabs)
    return pl.pallas_call(
        functools.partial(_proj_qk_kernel, layer=layer, col0=col0, nj=nj, ni=ni, head_dim=head_dim, rope=rope),
        grid=(nj, ni),
        in_specs=in_specs,
        out_specs=pl.BlockSpec((tm, tn), lambda j, i: (i, j)),
        out_shape=jax.ShapeDtypeStruct((n_rows, width), BF16),
        scratch_shapes=_w_scratch(k, tn, nk),
        compiler_params=_params("arbitrary", "arbitrary"),
        name="proj_qk",
    )(*args)


def _merge_kernel(oa_ref, ob_ref, oc_ref, wa_hbm, wb_hbm, wc_hbm, ga_ref, gb_ref, gc_ref, o_ref,
                  wa, sa, sema, wb, sb, semb, wc, sc, semc, *, layer, nj, ni):
    buf = _weight_pipeline([_WStream(wa_hbm, layer, 0, wa, sa, sema), _WStream(wb_hbm, layer, 0, wb, sb, semb),
                            _WStream(wc_hbm, layer, 0, wc, sc, semc)], nj, ni)
    for rows in _row_halves(o_ref):
        m = _sigmoid(ga_ref[rows, :].astype(F32)) * _dot(oa_ref[rows, :], wa[buf])
        m += _sigmoid(gb_ref[rows, :].astype(F32)) * _dot(ob_ref[rows, :], wb[buf])
        m += _sigmoid(gc_ref[rows, :].astype(F32)) * _dot(oc_ref[rows, :], wc[buf])
        o_ref[rows, :] = m.astype(o_ref.dtype)


def gated_merge(oa, ob, oc, wa, wb, wc, layer, gates, n_rows, tn=512):
    d = wa.shape[2]
    tn = min(tn, d)
    nj = d // tn
    tm = _big_row_tile(n_rows)
    ni = n_rows // tm
    nk = min(_n_chunks(w.shape[1], ni) for w in (wa, wb, wc))
    assert all(w.shape[1] % nk == 0 and (w.shape[1] // nk) % BF16_SUBLANES == 0 for w in (wa, wb, wc))
    act = lambda o: pl.BlockSpec((tm, o.shape[1]), lambda j, i: (i, 0))
    gate = lambda b: pl.BlockSpec((tm, tn), lambda j, i: (i, j + b * nj))
    return pl.pallas_call(
        functools.partial(_merge_kernel, layer=layer, nj=nj, ni=ni),
        grid=(nj, ni),
        in_specs=[act(oa), act(ob), act(oc), _HBM, _HBM, _HBM, gate(0), gate(1), gate(2)],
        out_specs=pl.BlockSpec((tm, tn), lambda j, i: (i, j)),
        out_shape=jax.ShapeDtypeStruct((n_rows, d), BF16),
        scratch_shapes=sum((_w_scratch(w.shape[1], tn, nk) for w in (wa, wb, wc)), []),
        compiler_params=_params("arbitrary", "arbitrary"),
        name="gated_merge",
    )(oa, ob, oc, wa, wb, wc, gates, gates, gates)


def _head_slices(n):
    return [slice(j * HEAD_DIM, (j + 1) * HEAD_DIM) for j in range(n)]


def _build_vaug(vaug_ref, vl_ref, vc_ref):
    seq = vl_ref.shape[0]
    for j, hs in enumerate(_head_slices(vaug_ref.shape[0])):
        vaug_ref[j, 0:seq, 0:HEAD_DIM] = vl_ref[:, hs]
        vaug_ref[j, seq:, 0:HEAD_DIM] = vc_ref[:, hs]
        vaug_ref[j, :, HEAD_DIM:] = jnp.ones((vaug_ref.shape[1], HEAD_DIM), vaug_ref.dtype)


def _attend(q, parts):
    acc = m_run = None
    for k, vaug, bias in parts:
        s = _dot_t(q, k)
        if bias is not None:
            s = s + bias
        m_new = jnp.max(s, axis=-1, keepdims=True)
        if acc is not None:
            m_new = jnp.maximum(m_run, m_new)
        o = _dot(jnp.exp2(s - m_new).astype(vaug.dtype), vaug)
        acc = o if acc is None else jnp.exp2(m_run - m_new) * acc + o
        m_run = m_new
    return acc[:, :HEAD_DIM] / acc[:, HEAD_DIM:]


def _key_parts(kl_ref, kc_ref, vaug_ref, latent, j, hs):
    seq = kl_ref.shape[0]
    parts = []
    if latent:
        for c in range(seq // KEY_CHUNK):
            sl = slice(c * KEY_CHUNK, (c + 1) * KEY_CHUNK)
            parts.append((kl_ref[sl, hs], vaug_ref[j, sl, :], None))
    parts.append((kc_ref[:, hs], vaug_ref[j, seq:, :], None))
    return parts


def _gqa_kernel(q_ref, kl_ref, kc_ref, vl_ref, vc_ref, o_ref, vaug_ref, *, group, n_lat, with_ctx):
    qt = pl.program_id(2)
    pl.when(qt == 0)(lambda: _build_vaug(vaug_ref, vl_ref, vc_ref))

    def run(latent):
        for j, hs in enumerate(_head_slices(vaug_ref.shape[0])):
            cols = [slice((j * group + g) * HEAD_DIM, (j * group + g + 1) * HEAD_DIM) for g in range(group)]
            q = jnp.concatenate([q_ref[:, c] for c in cols], axis=0)
            o = _attend(q, _key_parts(kl_ref, kc_ref, vaug_ref, latent, j, hs))
            for g, c in enumerate(cols):
                o_ref[:, c] = o[g * Q_TILE:(g + 1) * Q_TILE].astype(o_ref.dtype)

    if with_ctx:
        pl.when(qt < n_lat)(lambda: run(True))
        pl.when(qt >= n_lat)(lambda: run(False))
    else:
        run(True)


def _q_row_block(b, qt, n_lat, n_batch):
    return jnp.where(qt < n_lat, b * n_lat + qt, n_batch * n_lat + b)


def gqa_attention(q, k, v, n_batch, seq, ctx_len, group, with_ctx):
    rows = q.shape[0] if with_ctx else n_batch * seq
    kvh = k.shape[1] // HEAD_DIM
    n_lat = seq // Q_TILE
    ctx_blk0 = n_batch * seq // ctx_len
    nh = GQA_KV_HEADS if kvh % GQA_KV_HEADS == 0 else 1
    gw, kw = nh * group * HEAD_DIM, nh * HEAD_DIM
    qmap = lambda b, h, t: (_q_row_block(b, t, n_lat, n_batch), h)
    lat = pl.BlockSpec((seq, kw), lambda b, h, t: (b, h))
    ctx = pl.BlockSpec((ctx_len, kw), lambda b, h, t: (ctx_blk0 + b, h))
    return pl.pallas_call(
        functools.partial(_gqa_kernel, group=group, n_lat=n_lat, with_ctx=with_ctx),
        grid=(n_batch, kvh // nh, n_lat + int(with_ctx)),
        in_specs=[pl.BlockSpec((Q_TILE, gw), qmap), lat, ctx, lat, ctx],
        out_specs=pl.BlockSpec((Q_TILE, gw), qmap),
        out_shape=jax.ShapeDtypeStruct((rows, q.shape[1]), BF16),
        scratch_shapes=[pltpu.VMEM((nh, seq + ctx_len, 2 * HEAD_DIM), BF16)],
        compiler_params=_params("arbitrary", "arbitrary", "arbitrary"),
        name="gqa_attention",
    )(q, k, k, v, v)


def _diff_kernel(q_ref, kl_ref, kc_ref, vl_ref, vc_ref, lam_ref, g_ref, o_ref, vaug_ref, *,
                 lam_init, n_lat, with_ctx):
    qt = pl.program_id(2)
    pl.when(qt == 0)(lambda: _build_vaug(vaug_ref, vl_ref, vc_ref))
    lp = lam_ref[...]
    lam = (jnp.exp(jnp.sum(lp[0:1] * lp[1:2], axis=-1, keepdims=True))
           - jnp.exp(jnp.sum(lp[2:3] * lp[3:4], axis=-1, keepdims=True)) + lam_init)

    def run(latent):
        for j, hs in enumerate(_head_slices(vaug_ref.shape[0])):
            q = q_ref[:, hs]
            lane = lax.broadcasted_iota(jnp.int32, q.shape, 1)
            zero = jnp.zeros_like(q)
            q2 = jnp.concatenate([jnp.where(lane < C_QK_DIM, q, zero), jnp.where(lane < C_QK_DIM, zero, q)],
                                 axis=0)
            o2 = _attend(q2, _key_parts(kl_ref, kc_ref, vaug_ref, latent, j, hs))
            o = o2[:Q_TILE] - lam * o2[Q_TILE:]
            o = o * lax.rsqrt(jnp.mean(o * o, axis=-1, keepdims=True) + EPS) * g_ref[...]
            o_ref[:, hs] = (o * (1.0 - lam_init)).astype(o_ref.dtype)

    if with_ctx:
        pl.when(qt < n_lat)(lambda: run(True))
        pl.when(qt >= n_lat)(lambda: run(False))
    else:
        run(True)


def diff_attention(q, k, v, lam_params, subln, lam_init, n_batch, seq, ctx_len, with_ctx):
    rows = q.shape[0] if with_ctx else n_batch * seq
    n_heads = v.shape[1] // HEAD_DIM
    n_lat = seq // Q_TILE
    ctx_blk0 = n_batch * seq // ctx_len
    nh = DIFF_HEADS if n_heads % DIFF_HEADS == 0 else 1
    hw = nh * HEAD_DIM
    qmap = lambda b, h, t: (_q_row_block(b, t, n_lat, n_batch), h)
    lat = pl.BlockSpec((seq, hw), lambda b, h, t: (b, h))
    ctx = pl.BlockSpec((ctx_len, hw), lambda b, h, t: (ctx_blk0 + b, h))
    full = lambda a: pl.BlockSpec(a.shape, lambda b, h, t: (0,) * a.ndim)
    subln = subln.reshape(1, -1)
    return pl.pallas_call(
        functools.partial(_diff_kernel, lam_init=lam_init, n_lat=n_lat, with_ctx=with_ctx),
        grid=(n_batch, n_heads // nh, n_lat + int(with_ctx)),
        in_specs=[pl.BlockSpec((Q_TILE, hw), qmap), lat, ctx, lat, ctx, full(lam_params), full(subln)],
        out_specs=pl.BlockSpec((Q_TILE, hw), qmap),
        out_shape=jax.ShapeDtypeStruct((rows, v.shape[1]), BF16),
        scratch_shapes=[pltpu.VMEM((nh, seq + ctx_len, 2 * HEAD_DIM), BF16)],
        compiler_params=_params("arbitrary", "arbitrary", "arbitrary"),
        name="diff_attention",
    )(q, k, k, v, v, lam_params, subln)


def _na_plan(rows):
    kh = min(NA_KH, rows)
    band = min(NA_ROWS + kh, rows)
    patterns, types = [], []
    for blk in range(rows // NA_ROWS):
        r0 = blk * NA_ROWS
        bs = min(max(r0 - kh // 2, 0), rows - band)
        pat = []
        for qr in range(r0, r0 + NA_ROWS):
            rs = min(max(qr - kh // 2, 0), rows - kh)
            pat.append(tuple((kr - qr + kh - 1) if rs <= kr < rs + kh else -1 for kr in range(bs, bs + band)))
        pat = tuple(pat)
        if pat not in patterns:
            patterns.append(pat)
        types.append(patterns.index(pat))
    return band, np.asarray(types, np.int32), patterns


def _na_bias_kernel(rb_ref, o_ref, *, kh, patterns):
    h = pl.program_id(0)
    n_dc = 2 * NA_KW - 1
    base = h * ((2 * NA_KH - 1) * n_dc)
    qc = lax.broadcasted_iota(jnp.int32, (GRID_W, GRID_W), 0)
    kc = lax.broadcasted_iota(jnp.int32, (GRID_W, GRID_W), 1)
    dc = kc - qc
    cs = jnp.clip(qc - NA_KW // 2, 0, GRID_W - NA_KW)
    col_ok = (kc >= cs) & (kc < cs + NA_KW)
    neg = jnp.full((GRID_W, GRID_W), NEG_BIAS, F32)
    used = sorted({a for pat in patterns for row in pat for a in row if a >= 0})
    toep = {}
    for a in used:
        a_full = a + (NA_KH - kh)
        t = neg
        for b in range(n_dc):
            t = jnp.where(dc == b - (NA_KW - 1), rb_ref[base + a_full * n_dc + b] * LOG2E, t)
        toep[a] = jnp.where(col_ok, t, neg)
    for t_id, pat in enumerate(patterns):
        for qr, row in enumerate(pat):
            blocks = [toep[a] if a >= 0 else neg for a in row]
            o_ref[0, t_id, qr * GRID_W:(qr + 1) * GRID_W, :] = jnp.concatenate(blocks, axis=1)


def na_bias_tables(rel_bias, rows):
    n_heads = rel_bias.shape[0]
    kh = min(NA_KH, rows)
    band, _, patterns = _na_plan(rows)
    shape = (n_heads, len(patterns), Q_TILE, band * GRID_W)
    return pl.pallas_call(
        functools.partial(_na_bias_kernel, kh=kh, patterns=patterns),
        grid=(n_heads,),
        in_specs=[pl.BlockSpec(memory_space=pltpu.SMEM)],
        out_specs=pl.BlockSpec((1,) + shape[1:], lambda h: (h, 0, 0, 0)),
        out_shape=jax.ShapeDtypeStruct(shape, F32),
        compiler_params=_params("arbitrary"),
        name="na_bias_tables",
    )(rel_bias.reshape(-1))


def _na_kernel(types_ref, q_ref, kl_ref, kc_ref, vl_ref, vc_ref, bias_ref, o_ref, vaug_ref, *,
               rows, band, n_lat, with_ctx):
    del types_ref
    rt = pl.program_id(2)
    kh = min(NA_KH, rows)
    seq = kl_ref.shape[0]
    heads = _head_slices(NA_HEADS)
    pl.when(rt == 0)(lambda: _build_vaug(vaug_ref, vl_ref, vc_ref))
    ctx_part = lambda j, hs: (kc_ref[:, hs], vaug_ref[j, seq:, :], None)

    def latent():
        bs = jnp.clip(rt * NA_ROWS - kh // 2, 0, rows - band)
        band_sl = pl.ds(pl.multiple_of(bs * GRID_W, GRID_W), band * GRID_W)
        for j, hs in enumerate(heads):
            parts = [(kl_ref[band_sl, hs], vaug_ref[j, band_sl, :], bias_ref[j, 0]), ctx_part(j, hs)]
            o_ref[:, hs] = _attend(q_ref[:, hs], parts).astype(o_ref.dtype)

    def context():
        for j, hs in enumerate(heads):
            o_ref[:, hs] = _attend(q_ref[:, hs], [ctx_part(j, hs)]).astype(o_ref.dtype)

    if with_ctx:
        pl.when(rt < n_lat)(latent)
        pl.when(rt >= n_lat)(context)
    else:
        latent()


def neighbourhood_attention(q, k, v, bias_tabs, n_batch, seq, ctx_len, with_ctx):
    rows_tok = q.shape[0] if with_ctx else n_batch * seq
    n_heads = q.shape[1] // HEAD_DIM
    rows = seq // GRID_W
    band, types, _ = _na_plan(rows)
    n_lat = seq // Q_TILE
    ctx_blk0 = n_batch * seq // ctx_len
    types = jnp.asarray(np.concatenate([types, types[-1:]]))
    hw = NA_HEADS * HEAD_DIM
    assert n_heads % NA_HEADS == 0
    qmap = lambda b, h, t, ty: (_q_row_block(b, t, n_lat, n_batch), h)
    lat = pl.BlockSpec((seq, hw), lambda b, h, t, ty: (b, h))
    ctx = pl.BlockSpec((ctx_len, hw), lambda b, h, t, ty: (ctx_blk0 + b, h))
    bias = pl.BlockSpec((NA_HEADS, 1, Q_TILE, band * GRID_W), lambda b, h, t, ty: (h, ty[t], 0, 0))
    return pl.pallas_call(
        functools.partial(_na_kernel, rows=rows, band=band, n_lat=n_lat, with_ctx=with_ctx),
        grid_spec=pltpu.PrefetchScalarGridSpec(
            num_scalar_prefetch=1,
            grid=(n_batch, n_heads // NA_HEADS, n_lat + int(with_ctx)),
            in_specs=[pl.BlockSpec((Q_TILE, hw), qmap), lat, ctx, lat, ctx, bias],
            out_specs=pl.BlockSpec((Q_TILE, hw), qmap),
            scratch_shapes=[pltpu.VMEM((NA_HEADS, seq + ctx_len, 2 * HEAD_DIM), BF16)],
        ),
        out_shape=jax.ShapeDtypeStruct((rows_tok, q.shape[1]), BF16),
        compiler_params=_params("arbitrary", "arbitrary", "arbitrary"),
        name="neighbourhood_attention",
    )(types, q, k, k, v, v, bias_tabs)


def _rope_tables(seq, dim, n_batch, n_ctx_rows):
    t = jnp.arange(seq, dtype=jnp.int32)
    row = (t // GRID_W).astype(F32)
    col = (t % GRID_W).astype(F32)
    n_pairs = dim // 4
    inv = ROPE_THETA ** (-jnp.arange(n_pairs, dtype=F32) / n_pairs)
    ang = jnp.concatenate([row[:, None] * inv, col[:, None] * inv], axis=-1)
    cos = jnp.repeat(jnp.cos(ang), 2, axis=-1)
    sin = jnp.stack([-jnp.sin(ang), jnp.sin(ang)], axis=-1).reshape(seq, dim)
    reps = LANES // dim
    cos, sin = jnp.tile(cos, (n_batch, reps)), jnp.tile(sin, (n_batch, reps))
    cos = jnp.concatenate([cos, jnp.ones((n_ctx_rows, LANES), F32)], axis=0)
    sin = jnp.concatenate([sin, jnp.zeros((n_ctx_rows, LANES), F32)], axis=0)
    return cos, sin


def kernel(x, c, ctx, c_ctx, w_ada, b_ada, norm_g, ffn1_in, ffn1_out, ffn2_in, ffn2_out, w_in,
           qk_gain_a, qk_gain_b, qk_gain_c, na_rel_bias, diff_lambda, diff_subln,
           w_br_a, w_br_b, w_br_c, w_out):
    n_batch, seq, d = x.shape
    ctx_len = ctx.shape[1]
    depth = w_ada.shape[0]
    a_q_w, b_w, c_w = w_br_a.shape[1], w_br_b.shape[1], w_br_c.shape[1]
    a_kv_w = (w_in.shape[2] - a_q_w - 3 * b_w - 3 * c_w - 3 * d) // 2
    group = a_q_w // a_kv_w
    assert ctx_len == Q_TILE and seq % ROW_TILE == 0 and (n_batch * ctx_len) % ROW_TILE == 0
    assert seq % KEY_CHUNK == 0 and n_batch + 1 <= MOD_ROWS
    n_lat_rows = n_batch * seq
    n_ctx_rows = n_batch * ctx_len
    lat_tiles = n_lat_rows // ROW_TILE
    all_tiles = lat_tiles + n_ctx_rows // ROW_TILE
    n_all_rows = n_lat_rows + n_ctx_rows
    tiles_per_batch = seq // ROW_TILE
    seg_of_tile = lambda i: jnp.minimum(i // tiles_per_batch, n_batch)

    sizes = (a_q_w, a_kv_w, a_kv_w, b_w, b_w, b_w, c_w, c_w, c_w, 3 * d)
    offs = [int(o) for o in np.cumsum((0,) + sizes)]

    cvec = jnp.concatenate([c, c_ctx[None], jnp.zeros((MOD_ROWS - n_batch - 1, d), F32)], axis=0)
    mods_all = ada_modulation(cvec, w_ada, b_ada)

    rope_a = _rope_tables(seq, HEAD_DIM, n_batch, n_ctx_rows)
    rope_c = _rope_tables(seq, C_QK_DIM, n_batch, n_ctx_rows)
    rows = seq // GRID_W

    xs = (x.reshape(n_lat_rows, d), ctx.reshape(n_ctx_rows, d))

    w1i, w1o, w2i, w2o = ffn1_in, ffn1_out, ffn2_in, ffn2_out
    wp, wa, wb, wc, wo = w_in, w_br_a, w_br_b, w_br_c, w_out

    def gain_row(gain, width, scale):
        return jnp.tile(gain * scale, width // gain.shape[0]).reshape(1, width)

    sa, sc = HEAD_DIM ** -0.5 * LOG2E, C_QK_DIM ** -0.5 * LOG2E

    for l in range(depth):
        last = l == depth - 1
        with_ctx = not last
        lam_init = 0.8 - 0.6 * math.exp(-0.3 * l)
        mods = mods_all[l].reshape(MOD_ROWS, 1, N_MOD * d)

        h = norm_mod(xs, norm_g[l, 0], mods, 0, 1, seg_of_tile, all_tiles)
        a = ffn_up(h, w1i, l, n_all_rows)
        xs = resid_matmul(a, w1o, l, xs, mods, 2, 0.5, seq, n_batch, n_all_rows,
                          tn=FFN_DOWN_TN * (2 if isinstance(xs, tuple) else 1))

        h = norm_mod(xs, norm_g[l, 1], mods, 3, 4, seg_of_tile, all_tiles)
        qa = proj_qk(h, wp, l, offs[0], a_q_w, gain_row(qk_gain_a[l, 0], a_q_w, sa), HEAD_DIM, rope_a, n_all_rows)
        ka = proj_qk(h, wp, l, offs[1], a_kv_w, gain_row(qk_gain_a[l, 1], a_kv_w, 1.0), HEAD_DIM, rope_a, n_all_rows)
        va = proj_plain(h, wp, l, offs[2], a_kv_w, n_all_rows)
        qb = proj_qk(h, wp, l, offs[3], b_w, gain_row(qk_gain_b[l, 0], b_w, sa), HEAD_DIM, None, n_all_rows)
        kb = proj_qk(h, wp, l, offs[4], b_w, gain_row(qk_gain_b[l, 1], b_w, 1.0), HEAD_DIM, None, n_all_rows)
        vb = proj_plain(h, wp, l, offs[5], b_w, n_all_rows)
        qc = proj_qk(h, wp, l, offs[6], c_w, gain_row(qk_gain_c[l, 0], c_w, sc), C_QK_DIM, rope_c, n_all_rows)
        kc = proj_qk(h, wp, l, offs[7], c_w, gain_row(qk_gain_c[l, 1], c_w, 1.0), C_QK_DIM, rope_c, n_all_rows)
        vc = proj_plain(h, wp, l, offs[8], c_w, n_all_rows)
        mix_tiles = all_tiles if with_ctx else lat_tiles
        mix_rows = mix_tiles * ROW_TILE
        gates = proj_plain(h, wp, l, offs[9], 3 * d, mix_rows, tn=min(1024, d))

        o_a = gqa_attention(qa, ka, va, n_batch, seq, ctx_len, group, with_ctx)
        bias_tabs = na_bias_tables(na_rel_bias[l], rows)
        o_b = neighbourhood_attention(qb, kb, vb, bias_tabs, n_batch, seq, ctx_len, with_ctx)
        o_c = diff_attention(qc, kc, vc, diff_lambda[l], diff_subln[l], lam_init, n_batch, seq, ctx_len, with_ctx)

        m = gated_merge(o_a, o_b, o_c, wa, wb, wc, l, gates, mix_rows)
        xs = resid_matmul(m, wo, l, xs, mods, 5, 1.0, seq, n_batch, mix_rows, tn=2 * FFN_DOWN_TN)

        h = norm_mod(xs, norm_g[l, 2], mods, 6, 7, seg_of_tile, mix_tiles)
        a = ffn_up(h, w2i, l, mix_rows)
        xs = resid_matmul(a, w2o, l, xs, mods, 8, 0.5, seq, n_batch, mix_rows, tn=FFN_DOWN_TN)

    return xs.reshape(n_batch, seq, d)
```

```python
import functools
import math
import typing

import numpy as np
import jax
import jax.numpy as jnp
from jax import lax
from jax.experimental import pallas as pl
from jax.experimental.pallas import tpu as pltpu

F32 = jnp.float32
BF16 = jnp.bfloat16

GRID_W = 64
HEAD_DIM = 128
C_QK_DIM = 64
NA_KH = 8
NA_KW = 16
ROPE_THETA = 10000.0
EPS = 1e-6
N_MOD = 9
LANES = 128
MOD_ROWS = 8
NEG_BIAS = -1e30
LOG2E = math.log2(math.e)
KEY_CHUNK = 512

V7X_VMEM_BYTES = 64 * 1024 * 1024
VMEM_LIMIT = (V7X_VMEM_BYTES * 7) // 8

ROW_TILE = 512
BF16_SUBLANES = 16
MAX_W_CHUNKS = 8
W_STAGE_SLOTS = 2
QK_SUB_BLOCKS = 4
MAX_ROW_TILE = 1088
NA_HEADS = 4
DIFF_HEADS = 4
GQA_KV_HEADS = 4
Q_TILE = 256
NA_ROWS = Q_TILE // GRID_W


def _params(*sem):
    return pltpu.CompilerParams(dimension_semantics=sem, vmem_limit_bytes=VMEM_LIMIT)


def _sigmoid(v):
    return 1.0 / (1.0 + jnp.exp(-v))


def _dot(a, b):
    return jnp.dot(a, b, preferred_element_type=F32)


def _dot_t(a, b):
    return lax.dot_general(a, b, (((1,), (1,)), ((), ())), preferred_element_type=F32)


def _ada_kernel(c_ref, w_ref, b_ref, o_ref):
    c = c_ref[...]
    a = (c * _sigmoid(c)).astype(BF16)
    o_ref[0] = _dot(a, w_ref[0].astype(BF16)) + b_ref[0]


def ada_modulation(cvec, w_ada, b_ada, tn=512):
    depth, d, n = w_ada.shape
    return pl.pallas_call(
        _ada_kernel,
        grid=(depth, n // tn),
        in_specs=[
            pl.BlockSpec((MOD_ROWS, d), lambda l, j: (0, 0)),
            pl.BlockSpec((1, d, tn), lambda l, j: (l, 0, j)),
            pl.BlockSpec((1, 1, tn), lambda l, j: (l, 0, j)),
        ],
        out_specs=pl.BlockSpec((1, MOD_ROWS, tn), lambda l, j: (l, 0, j)),
        out_shape=jax.ShapeDtypeStruct((depth, MOD_ROWS, n), F32),
        compiler_params=_params("arbitrary", "arbitrary"),
        name="ada_mod",
    )(cvec, w_ada, b_ada.reshape(depth, 1, n))


def _row_sources(x, n_tiles, block_cols, col_of):
    if not isinstance(x, tuple):
        return [x], [pl.BlockSpec((ROW_TILE, block_cols), lambda *g: (g[-1], col_of(*g)))], None
    n_first = x[0].shape[0] // ROW_TILE
    assert x[0].shape[0] % ROW_TILE == 0 and x[1].shape[0] == (n_tiles - n_first) * ROW_TILE
    specs = [pl.BlockSpec((ROW_TILE, block_cols), lambda *g: (jnp.minimum(g[-1], n_first - 1), col_of(*g))),
             pl.BlockSpec((ROW_TILE, block_cols), lambda *g: (jnp.maximum(g[-1] - n_first, 0), col_of(*g)))]
    return list(x), specs, n_first


def _pick_rows(x_refs, n_first, i):
    if n_first is None:
        return x_refs[0][...]
    return jnp.where(i < n_first, x_refs[0][...], x_refs[1][...])


def _normmod_kernel(*refs, n_first):
    *x_refs, g_ref, shift_ref, scale_ref, o_ref = refs
    x = _pick_rows(x_refs, n_first, pl.program_id(0))
    y = x * lax.rsqrt(jnp.mean(x * x, axis=-1, keepdims=True) + EPS) * g_ref[...]
    o_ref[...] = (y * (1.0 + scale_ref[0]) + shift_ref[0]).astype(o_ref.dtype)


def norm_mod(x, g, mods, shift_idx, scale_idx, seg_of_tile, n_tiles):
    d = g.shape[0]
    xs, x_specs, n_first = _row_sources(x, n_tiles, d, lambda i: 0)
    return pl.pallas_call(
        functools.partial(_normmod_kernel, n_first=n_first),
        grid=(n_tiles,),
        in_specs=x_specs + [
            pl.BlockSpec((1, d), lambda i: (0, 0)),
            pl.BlockSpec((1, 1, d), lambda i: (seg_of_tile(i), 0, shift_idx)),
            pl.BlockSpec((1, 1, d), lambda i: (seg_of_tile(i), 0, scale_idx)),
        ],
        out_specs=pl.BlockSpec((ROW_TILE, d), lambda i: (i, 0)),
        out_shape=jax.ShapeDtypeStruct((n_tiles * ROW_TILE, d), BF16),
        compiler_params=_params("arbitrary"),
        name="norm_mod",
    )(*xs, g.reshape(1, d), mods, mods)


class _WStream(typing.NamedTuple):
    hbm: typing.Any
    layer: int
    col0: int
    wbuf: typing.Any
    stage: typing.Any
    sem: typing.Any


def _n_chunks(k, n_row_tiles):
    for nk in range(min(MAX_W_CHUNKS, W_STAGE_SLOTS * (n_row_tiles - 1)), 0, -1):
        if k % nk == 0 and (k // nk) % BF16_SUBLANES == 0:
            return nk
    raise ValueError(f"no weight chunking for K={k} with {n_row_tiles} row tiles")


def _big_row_tile(n_rows):
    for tm in range(MAX_ROW_TILE, 0, -QK_SUB_BLOCKS * BF16_SUBLANES):
        if n_rows % tm == 0:
            return tm
    raise ValueError(f"no row tile for {n_rows} rows")


def _w_copy(st, tile, chunk, slot):
    _, ck, tn = st.stage.shape
    rows = pl.ds(pl.multiple_of(chunk * ck, BF16_SUBLANES), ck)
    cols = pl.ds(pl.multiple_of(st.col0 + tile * tn, LANES), tn)
    return pltpu.make_async_copy(st.hbm.at[st.layer, rows, cols], st.stage.at[slot], st.sem.at[slot])


def _w_round(st, buf, chunk, slot):
    ck = st.stage.shape[1]
    rows = pl.ds(pl.multiple_of(chunk * ck, BF16_SUBLANES), ck)
    st.wbuf[buf, rows, :] = st.stage[slot].astype(st.wbuf.dtype)


def _weight_pipeline(streams, nj, ni):
    j, i = pl.program_id(0), pl.program_id(1)
    nk = streams[0].wbuf.shape[1] // streams[0].stage.shape[1]
    cps = max(1, -(-nk // max(ni - 1, 1)))
    assert cps <= W_STAGE_SLOTS and (nj == 1 or (ni - 1) * cps >= nk)

    @pl.when((j == 0) & (i == 0))
    def _prime():
        for st in streams:
            _w_copy(st, 0, 0, 0).start()
        for c in range(nk):
            for st in streams:
                if c + 1 < nk:
                    _w_copy(st, 0, c + 1, (c + 1) % W_STAGE_SLOTS).start()
                _w_copy(st, 0, c, c % W_STAGE_SLOTS).wait()
                _w_round(st, 0, c, c % W_STAGE_SLOTS)

    @pl.when(j + 1 < nj)
    def _prefetch():
        for s in range(cps):
            prev = (i - 1) * cps + s

            @pl.when((i >= 1) & (prev < nk))
            def _(prev=prev, s=s):
                for st in streams:
                    _w_copy(st, j + 1, prev, s).wait()
                    _w_round(st, (j + 1) % 2, prev, s)

        for s in range(cps):
            cur = i * cps + s

            @pl.when(cur < nk)
            def _(cur=cur, s=s):
                for st in streams:
                    _w_copy(st, j + 1, cur, s).start()

    return j % 2


def _w_scratch(k, tn, nk):
    return [pltpu.VMEM((2, k, tn), BF16), pltpu.VMEM((W_STAGE_SLOTS, k // nk, tn), F32),
            pltpu.SemaphoreType.DMA((W_STAGE_SLOTS,))]


_HBM = pl.BlockSpec(memory_space=pl.ANY)


def _row_halves(ref, n=2):
    sub = ref.shape[0] // n
    assert sub * n == ref.shape[0] and sub % BF16_SUBLANES == 0
    return tuple(slice(r * sub, (r + 1) * sub) for r in range(n))


def _swiglu_kernel(a_ref, w_hbm, o_ref, wg, sg, semg, wu, su, semu, *, layer, nj, ni):
    tn = wg.shape[2]
    buf = _weight_pipeline([_WStream(w_hbm, layer, 0, wg, sg, semg),
                            _WStream(w_hbm, layer, nj * tn, wu, su, semu)], nj, ni)
    for rows in _row_halves(a_ref):
        a = a_ref[rows, :]
        g = _dot(a, wg[buf])
        u = _dot(a, wu[buf])
        o_ref[rows, :] = (g * _sigmoid(g) * u).astype(o_ref.dtype)


def ffn_up(h, w_in, layer, n_rows, tn=512):
    k = h.shape[1]
    f = w_in.shape[2] // 2
    nj = f // tn
    tm = _big_row_tile(n_rows)
    ni = n_rows // tm
    nk = _n_chunks(k, ni)
    return pl.pallas_call(
        functools.partial(_swiglu_kernel, layer=layer, nj=nj, ni=ni),
        grid=(nj, ni),
        in_specs=[pl.BlockSpec((tm, k), lambda j, i: (i, 0)), _HBM],
        out_specs=pl.BlockSpec((tm, tn), lambda j, i: (i, j)),
        out_shape=jax.ShapeDtypeStruct((n_rows, f), BF16),
        scratch_shapes=_w_scratch(k, tn, nk) + _w_scratch(k, tn, nk),
        compiler_params=_params("arbitrary", "arbitrary"),
        name="ffn_up",
    )(h, w_in)


def _resid_kernel(a_ref, w_hbm, *refs, layer, nj, ni, coef, n_first):
    *x_refs, gate_ref, o_ref, wb, sb, semb = refs
    buf = _weight_pipeline([_WStream(w_hbm, layer, 0, wb, sb, semb)], nj, ni)
    y = _dot(a_ref[...], wb[buf])
    o_ref[...] = _pick_rows(x_refs, n_first, pl.program_id(1)) + (coef * gate_ref[0]) * y


def resid_matmul(a, w, layer, x, mods, gate_idx, coef, seg_of_tile, n_tiles, tn):
    _, k, n = w.shape
    tn = min(tn, n)
    nj = n // tn
    nk = _n_chunks(k, n_tiles)
    xs, x_specs, n_first = _row_sources(x, n_tiles, tn, lambda j, i: j)
    return pl.pallas_call(
        functools.partial(_resid_kernel, layer=layer, nj=nj, ni=n_tiles, coef=coef, n_first=n_first),
        grid=(nj, n_tiles),
        in_specs=[pl.BlockSpec((ROW_TILE, k), lambda j, i: (i, 0)), _HBM] + x_specs + [
            pl.BlockSpec((1, 1, tn), lambda j, i: (seg_of_tile(i), 0, gate_idx * nj + j)),
        ],
        out_specs=pl.BlockSpec((ROW_TILE, tn), lambda j, i: (i, j)),
        out_shape=jax.ShapeDtypeStruct((n_tiles * ROW_TILE, n), F32),
        scratch_shapes=_w_scratch(k, tn, nk),
        compiler_params=_params("arbitrary", "arbitrary"),
        name="resid_matmul",
    )(a, w, *xs, mods)


def _proj_kernel(a_ref, w_hbm, o_ref, wb, sb, semb, *, layer, col0, nj, ni):
    buf = _weight_pipeline([_WStream(w_hbm, layer, col0, wb, sb, semb)], nj, ni)
    for rows in _row_halves(a_ref):
        o_ref[rows, :] = _dot(a_ref[rows, :], wb[buf]).astype(o_ref.dtype)


def proj_plain(h, w, layer, col0, width, n_rows, tn=512):
    k = h.shape[1]
    nj = width // tn
    tm = _big_row_tile(n_rows)
    ni = n_rows // tm
    nk = _n_chunks(k, ni)
    return pl.pallas_call(
        functools.partial(_proj_kernel, layer=layer, col0=col0, nj=nj, ni=ni),
        grid=(nj, ni),
        in_specs=[pl.BlockSpec((tm, k), lambda j, i: (i, 0)), _HBM],
        out_specs=pl.BlockSpec((tm, tn), lambda j, i: (i, j)),
        out_shape=jax.ShapeDtypeStruct((n_rows, width), BF16),
        scratch_shapes=_w_scratch(k, tn, nk),
        compiler_params=_params("arbitrary", "arbitrary"),
        name="proj_plain",
    )(h, w)


def _swap_pairs(y):
    lane = lax.broadcasted_iota(jnp.int32, y.shape, 1)
    nxt = pltpu.roll(y, LANES - 1, 1)
    prv = pltpu.roll(y, 1, 1)
    return jnp.where(lane % 2 == 0, nxt, prv)


def _proj_qk_kernel(a_ref, w_hbm, gain_ref, *rest, layer, col0, nj, ni, head_dim, rope):
    if rope:
        cos_ref, sin_ref, o_ref, wb, sb, semb = rest
    else:
        o_ref, wb, sb, semb = rest
    buf = _weight_pipeline([_WStream(w_hbm, layer, col0, wb, sb, semb)], nj, ni)
    tn = wb.shape[2]
    for rows in _row_halves(a_ref, QK_SUB_BLOCKS):
        acc = _dot(a_ref[rows, :], wb[buf])
        for s in range(tn // LANES):
            sl = slice(s * LANES, (s + 1) * LANES)
            y = acc[:, sl]
            sq = y * y
            if head_dim == LANES:
                ms = jnp.mean(sq, axis=-1, keepdims=True)
            else:
                lane = lax.broadcasted_iota(jnp.int32, y.shape, 1)
                low = lane < head_dim
                s_low = jnp.sum(jnp.where(low, sq, 0.0), axis=-1, keepdims=True)
                s_high = jnp.sum(jnp.where(low, 0.0, sq), axis=-1, keepdims=True)
                ms = jnp.where(low, s_low, s_high) * (1.0 / head_dim)
            y = y * lax.rsqrt(ms + EPS) * gain_ref[:, sl]
            if rope:
                y = y * cos_ref[rows, :] + _swap_pairs(y) * sin_ref[rows, :]
            o_ref[rows, sl] = y.astype(o_ref.dtype)


def proj_qk(h, w, layer, col0, width, gain_row, head_dim, rope_tabs, n_rows, tn=512):
    k = h.shape[1]
    nj = width // tn
    tm = _big_row_tile(n_rows)
    ni = n_rows // tm
    nk = _n_chunks(k, ni)
    rope = rope_tabs is not None
    in_specs = [
        pl.BlockSpec((tm, k), lambda j, i: (i, 0)),
        _HBM,
        pl.BlockSpec((1, tn), lambda j, i: (0, j)),
    ]
    args = [h, w, gain_row]
    if rope:
        in_specs += [pl.BlockSpec((tm, LANES), lambda j, i: (i, 0))] * 2
        args += list(rope_tabs)
    return pl.pallas_call(
        functools.partial(_proj_qk_kernel, layer=layer, col0=col0, nj=nj, ni=ni, head_dim=head_dim, rope=rope),
        grid=(nj, ni),
        in_specs=in_specs,
        out_specs=pl.BlockSpec((tm, tn), lambda j, i: (i, j)),
        out_shape=jax.ShapeDtypeStruct((n_rows, width), BF16),
        scratch_shapes=_w_scratch(k, tn, nk),
        compiler_params=_params("arbitrary", "arbitrary"),
        name="proj_qk",
    )(*args)


def _merge_kernel(oa_ref, ob_ref, oc_ref, wa_hbm, wb_hbm, wc_hbm, ga_ref, gb_ref, gc_ref, o_ref,
                  wa, sa, sema, wb, sb, semb, wc, sc, semc, *, layer, nj, ni):
    buf = _weight_pipeline([_WStream(wa_hbm, layer, 0, wa, sa, sema), _WStream(wb_hbm, layer, 0, wb, sb, semb),
                            _WStream(wc_hbm, layer, 0, wc, sc, semc)], nj, ni)
    m = _sigmoid(ga_ref[...].astype(F32)) * _dot(oa_ref[...], wa[buf])
    m += _sigmoid(gb_ref[...].astype(F32)) * _dot(ob_ref[...], wb[buf])
    m += _sigmoid(gc_ref[...].astype(F32)) * _dot(oc_ref[...], wc[buf])
    o_ref[...] = m.astype(o_ref.dtype)


def gated_merge(oa, ob, oc, wa, wb, wc, layer, gates, n_tiles, tn=1024):
    d = wa.shape[2]
    tn = min(tn, d)
    nj = d // tn
    nk = min(_n_chunks(w.shape[1], n_tiles) for w in (wa, wb, wc))
    assert all(w.shape[1] % nk == 0 and (w.shape[1] // nk) % BF16_SUBLANES == 0 for w in (wa, wb, wc))
    act = lambda o: pl.BlockSpec((ROW_TILE, o.shape[1]), lambda j, i: (i, 0))
    gate = lambda b: pl.BlockSpec((ROW_TILE, tn), lambda j, i: (i, j + b * nj))
    return pl.pallas_call(
        functools.partial(_merge_kernel, layer=layer, nj=nj, ni=n_tiles),
        grid=(nj, n_tiles),
        in_specs=[act(oa), act(ob), act(oc), _HBM, _HBM, _HBM, gate(0), gate(1), gate(2)],
        out_specs=pl.BlockSpec((ROW_TILE, tn), lambda j, i: (i, j)),
        out_shape=jax.ShapeDtypeStruct((n_tiles * ROW_TILE, d), BF16),
        scratch_shapes=sum((_w_scratch(w.shape[1], tn, nk) for w in (wa, wb, wc)), []),
        compiler_params=_params("arbitrary", "arbitrary"),
        name="gated_merge",
    )(oa, ob, oc, wa, wb, wc, gates, gates, gates)


def _head_slices(n):
    return [slice(j * HEAD_DIM, (j + 1) * HEAD_DIM) for j in range(n)]


def _build_vaug(vaug_ref, vl_ref, vc_ref):
    seq = vl_ref.shape[0]
    for j, hs in enumerate(_head_slices(vaug_ref.shape[0])):
        vaug_ref[j, 0:seq, 0:HEAD_DIM] = vl_ref[:, hs]
        vaug_ref[j, seq:, 0:HEAD_DIM] = vc_ref[:, hs]
        vaug_ref[j, :, HEAD_DIM:] = jnp.ones((vaug_ref.shape[1], HEAD_DIM), vaug_ref.dtype)


def _attend(q, parts):
    acc = m_run = None
    for k, vaug, bias in parts:
        s = _dot_t(q, k)
        if bias is not None:
            s = s + bias
        m_new = jnp.max(s, axis=-1, keepdims=True)
        if acc is not None:
            m_new = jnp.maximum(m_run, m_new)
        o = _dot(jnp.exp2(s - m_new).astype(vaug.dtype), vaug)
        acc = o if acc is None else jnp.exp2(m_run - m_new) * acc + o
        m_run = m_new
    return acc[:, :HEAD_DIM] / acc[:, HEAD_DIM:]


def _key_parts(kl_ref, kc_ref, vaug_ref, latent, j, hs):
    seq = kl_ref.shape[0]
    parts = []
    if latent:
        for c in range(seq // KEY_CHUNK):
            sl = slice(c * KEY_CHUNK, (c + 1) * KEY_CHUNK)
            parts.append((kl_ref[sl, hs], vaug_ref[j, sl, :], None))
    parts.append((kc_ref[:, hs], vaug_ref[j, seq:, :], None))
    return parts


def _gqa_kernel(q_ref, kl_ref, kc_ref, vl_ref, vc_ref, o_ref, vaug_ref, *, group, n_lat, with_ctx):
    qt = pl.program_id(2)
    pl.when(qt == 0)(lambda: _build_vaug(vaug_ref, vl_ref, vc_ref))

    def run(latent):
        for j, hs in enumerate(_head_slices(vaug_ref.shape[0])):
            cols = [slice((j * group + g) * HEAD_DIM, (j * group + g + 1) * HEAD_DIM) for g in range(group)]
            q = jnp.concatenate([q_ref[:, c] for c in cols], axis=0)
            o = _attend(q, _key_parts(kl_ref, kc_ref, vaug_ref, latent, j, hs))
            for g, c in enumerate(cols):
                o_ref[:, c] = o[g * Q_TILE:(g + 1) * Q_TILE].astype(o_ref.dtype)

    if with_ctx:
        pl.when(qt < n_lat)(lambda: run(True))
        pl.when(qt >= n_lat)(lambda: run(False))
    else:
        run(True)


def _q_row_block(b, qt, n_lat, n_batch):
    return jnp.where(qt < n_lat, b * n_lat + qt, n_batch * n_lat + b)


def gqa_attention(q, k, v, n_batch, seq, ctx_len, group, with_ctx):
    rows = q.shape[0] if with_ctx else n_batch * seq
    kvh = k.shape[1] // HEAD_DIM
    n_lat = seq // Q_TILE
    ctx_blk0 = n_batch * seq // ctx_len
    nh = GQA_KV_HEADS if kvh % GQA_KV_HEADS == 0 else 1
    gw, kw = nh * group * HEAD_DIM, nh * HEAD_DIM
    qmap = lambda b, h, t: (_q_row_block(b, t, n_lat, n_batch), h)
    lat = pl.BlockSpec((seq, kw), lambda b, h, t: (b, h))
    ctx = pl.BlockSpec((ctx_len, kw), lambda b, h, t: (ctx_blk0 + b, h))
    return pl.pallas_call(
        functools.partial(_gqa_kernel, group=group, n_lat=n_lat, with_ctx=with_ctx),
        grid=(n_batch, kvh // nh, n_lat + int(with_ctx)),
        in_specs=[pl.BlockSpec((Q_TILE, gw), qmap), lat, ctx, lat, ctx],
        out_specs=pl.BlockSpec((Q_TILE, gw), qmap),
        out_shape=jax.ShapeDtypeStruct((rows, q.shape[1]), BF16),
        scratch_shapes=[pltpu.VMEM((nh, seq + ctx_len, 2 * HEAD_DIM), BF16)],
        compiler_params=_params("arbitrary", "arbitrary", "arbitrary"),
        name="gqa_attention",
    )(q, k, k, v, v)


def _diff_kernel(q_ref, kl_ref, kc_ref, vl_ref, vc_ref, lam_ref, g_ref, o_ref, vaug_ref, *,
                 lam_init, n_lat, with_ctx):
    qt = pl.program_id(2)
    pl.when(qt == 0)(lambda: _build_vaug(vaug_ref, vl_ref, vc_ref))
    lp = lam_ref[...]
    lam = (jnp.exp(jnp.sum(lp[0:1] * lp[1:2], axis=-1, keepdims=True))
           - jnp.exp(jnp.sum(lp[2:3] * lp[3:4], axis=-1, keepdims=True)) + lam_init)

    def run(latent):
        for j, hs in enumerate(_head_slices(vaug_ref.shape[0])):
            q = q_ref[:, hs]
            lane = lax.broadcasted_iota(jnp.int32, q.shape, 1)
            zero = jnp.zeros_like(q)
            q2 = jnp.concatenate([jnp.where(lane < C_QK_DIM, q, zero), jnp.where(lane < C_QK_DIM, zero, q)],
                                 axis=0)
            o2 = _attend(q2, _key_parts(kl_ref, kc_ref, vaug_ref, latent, j, hs))
            o = o2[:Q_TILE] - lam * o2[Q_TILE:]
            o = o * lax.rsqrt(jnp.mean(o * o, axis=-1, keepdims=True) + EPS) * g_ref[...]
            o_ref[:, hs] = (o * (1.0 - lam_init)).astype(o_ref.dtype)

    if with_ctx:
        pl.when(qt < n_lat)(lambda: run(True))
        pl.when(qt >= n_lat)(lambda: run(False))
    else:
        run(True)


def diff_attention(q, k, v, lam_params, subln, lam_init, n_batch, seq, ctx_len, with_ctx):
    rows = q.shape[0] if with_ctx else n_batch * seq
    n_heads = v.shape[1] // HEAD_DIM
    n_lat = seq // Q_TILE
    ctx_blk0 = n_batch * seq // ctx_len
    nh = DIFF_HEADS if n_heads % DIFF_HEADS == 0 else 1
    hw = nh * HEAD_DIM
    qmap = lambda b, h, t: (_q_row_block(b, t, n_lat, n_batch), h)
    lat = pl.BlockSpec((seq, hw), lambda b, h, t: (b, h))
    ctx = pl.BlockSpec((ctx_len, hw), lambda b, h, t: (ctx_blk0 + b, h))
    full = lambda a: pl.BlockSpec(a.shape, lambda b, h, t: (0,) * a.ndim)
    subln = subln.reshape(1, -1)
    return pl.pallas_call(
        functools.partial(_diff_kernel, lam_init=lam_init, n_lat=n_lat, with_ctx=with_ctx),
        grid=(n_batch, n_heads // nh, n_lat + int(with_ctx)),
        in_specs=[pl.BlockSpec((Q_TILE, hw), qmap), lat, ctx, lat, ctx, full(lam_params), full(subln)],
        out_specs=pl.BlockSpec((Q_TILE, hw), qmap),
        out_shape=jax.ShapeDtypeStruct((rows, v.shape[1]), BF16),
        scratch_shapes=[pltpu.VMEM((nh, seq + ctx_len, 2 * HEAD_DIM), BF16)],
        compiler_params=_params("arbitrary", "arbitrary", "arbitrary"),
        name="diff_attention",
    )(q, k, k, v, v, lam_params, subln)


def _na_plan(rows):
    kh = min(NA_KH, rows)
    band = min(NA_ROWS + kh, rows)
    patterns, types = [], []
    for blk in range(rows // NA_ROWS):
        r0 = blk * NA_ROWS
        bs = min(max(r0 - kh // 2, 0), rows - band)
        pat = []
        for qr in range(r0, r0 + NA_ROWS):
            rs = min(max(qr - kh // 2, 0), rows - kh)
            pat.append(tuple((kr - qr + kh - 1) if rs <= kr < rs + kh else -1 for kr in range(bs, bs + band)))
        pat = tuple(pat)
        if pat not in patterns:
            patterns.append(pat)
        types.append(patterns.index(pat))
    return band, np.asarray(types, np.int32), patterns


def _na_bias_kernel(rb_ref, o_ref, *, kh, patterns):
    h = pl.program_id(0)
    n_dc = 2 * NA_KW - 1
    base = h * ((2 * NA_KH - 1) * n_dc)
    qc = lax.broadcasted_iota(jnp.int32, (GRID_W, GRID_W), 0)
    kc = lax.broadcasted_iota(jnp.int32, (GRID_W, GRID_W), 1)
    dc = kc - qc
    cs = jnp.clip(qc - NA_KW // 2, 0, GRID_W - NA_KW)
    col_ok = (kc >= cs) & (kc < cs + NA_KW)
    neg = jnp.full((GRID_W, GRID_W), NEG_BIAS, F32)
    used = sorted({a for pat in patterns for row in pat for a in row if a >= 0})
    toep = {}
    for a in used:
        a_full = a + (NA_KH - kh)
        t = neg
        for b in range(n_dc):
            t = jnp.where(dc == b - (NA_KW - 1), rb_ref[base + a_full * n_dc + b] * LOG2E, t)
        toep[a] = jnp.where(col_ok, t, neg)
    for t_id, pat in enumerate(patterns):
        for qr, row in enumerate(pat):
            blocks = [toep[a] if a >= 0 else neg for a in row]
            o_ref[0, t_id, qr * GRID_W:(qr + 1) * GRID_W, :] = jnp.concatenate(blocks, axis=1)


def na_bias_tables(rel_bias, rows):
    n_heads = rel_bias.shape[0]
    kh = min(NA_KH, rows)
    band, _, patterns = _na_plan(rows)
    shape = (n_heads, len(patterns), Q_TILE, band * GRID_W)
    return pl.pallas_call(
        functools.partial(_na_bias_kernel, kh=kh, patterns=patterns),
        grid=(n_heads,),
        in_specs=[pl.BlockSpec(memory_space=pltpu.SMEM)],
        out_specs=pl.BlockSpec((1,) + shape[1:], lambda h: (h, 0, 0, 0)),
        out_shape=jax.ShapeDtypeStruct(shape, F32),
        compiler_params=_params("arbitrary"),
        name="na_bias_tables",
    )(rel_bias.reshape(-1))


def _na_kernel(types_ref, q_ref, kl_ref, kc_ref, vl_ref, vc_ref, bias_ref, o_ref, vaug_ref, *,
               rows, band, n_lat, with_ctx):
    del types_ref
    rt = pl.program_id(2)
    kh = min(NA_KH, rows)
    seq = kl_ref.shape[0]
    heads = _head_slices(NA_HEADS)
    pl.when(rt == 0)(lambda: _build_vaug(vaug_ref, vl_ref, vc_ref))
    ctx_part = lambda j, hs: (kc_ref[:, hs], vaug_ref[j, seq:, :], None)

    def latent():
        bs = jnp.clip(rt * NA_ROWS - kh // 2, 0, rows - band)
        band_sl = pl.ds(pl.multiple_of(bs * GRID_W, GRID_W), band * GRID_W)
        for j, hs in enumerate(heads):
            parts = [(kl_ref[band_sl, hs], vaug_ref[j, band_sl, :], bias_ref[j, 0]), ctx_part(j, hs)]
            o_ref[:, hs] = _attend(q_ref[:, hs], parts).astype(o_ref.dtype)

    def context():
        for j, hs in enumerate(heads):
            o_ref[:, hs] = _attend(q_ref[:, hs], [ctx_part(j, hs)]).astype(o_ref.dtype)

    if with_ctx:
        pl.when(rt < n_lat)(latent)
        pl.when(rt >= n_lat)(context)
    else:
        latent()


def neighbourhood_attention(q, k, v, bias_tabs, n_batch, seq, ctx_len, with_ctx):
    rows_tok = q.shape[0] if with_ctx else n_batch * seq
    n_heads = q.shape[1] // HEAD_DIM
    rows = seq // GRID_W
    band, types, _ = _na_plan(rows)
    n_lat = seq // Q_TILE
    ctx_blk0 = n_batch * seq // ctx_len
    types = jnp.asarray(np.concatenate([types, types[-1:]]))
    hw = NA_HEADS * HEAD_DIM
    assert n_heads % NA_HEADS == 0
    qmap = lambda b, h, t, ty: (_q_row_block(b, t, n_lat, n_batch), h)
    lat = pl.BlockSpec((seq, hw), lambda b, h, t, ty: (b, h))
    ctx = pl.BlockSpec((ctx_len, hw), lambda b, h, t, ty: (ctx_blk0 + b, h))
    bias = pl.BlockSpec((NA_HEADS, 1, Q_TILE, band * GRID_W), lambda b, h, t, ty: (h, ty[t], 0, 0))
    return pl.pallas_call(
        functools.partial(_na_kernel, rows=rows, band=band, n_lat=n_lat, with_ctx=with_ctx),
        grid_spec=pltpu.PrefetchScalarGridSpec(
            num_scalar_prefetch=1,
            grid=(n_batch, n_heads // NA_HEADS, n_lat + int(with_ctx)),
            in_specs=[pl.BlockSpec((Q_TILE, hw), qmap), lat, ctx, lat, ctx, bias],
            out_specs=pl.BlockSpec((Q_TILE, hw), qmap),
            scratch_shapes=[pltpu.VMEM((NA_HEADS, seq + ctx_len, 2 * HEAD_DIM), BF16)],
        ),
        out_shape=jax.ShapeDtypeStruct((rows_tok, q.shape[1]), BF16),
        compiler_params=_params("arbitrary", "arbitrary", "arbitrary"),
        name="neighbourhood_attention",
    )(types, q, k, k, v, v, bias_tabs)


def _rope_tables(seq, dim, n_batch, n_ctx_rows):
    t = jnp.arange(seq, dtype=jnp.int32)
    row = (t // GRID_W).astype(F32)
    col = (t % GRID_W).astype(F32)
    n_pairs = dim // 4
    inv = ROPE_THETA ** (-jnp.arange(n_pairs, dtype=F32) / n_pairs)
    ang = jnp.concatenate([row[:, None] * inv, col[:, None] * inv], axis=-1)
    cos = jnp.repeat(jnp.cos(ang), 2, axis=-1)
    sin = jnp.stack([-jnp.sin(ang), jnp.sin(ang)], axis=-1).reshape(seq, dim)
    reps = LANES // dim
    cos, sin = jnp.tile(cos, (n_batch, reps)), jnp.tile(sin, (n_batch, reps))
    cos = jnp.concatenate([cos, jnp.ones((n_ctx_rows, LANES), F32)], axis=0)
    sin = jnp.concatenate([sin, jnp.zeros((n_ctx_rows, LANES), F32)], axis=0)
    return cos, sin


def kernel(x, c, ctx, c_ctx, w_ada, b_ada, norm_g, ffn1_in, ffn1_out, ffn2_in, ffn2_out, w_in,
           qk_gain_a, qk_gain_b, qk_gain_c, na_rel_bias, diff_lambda, diff_subln,
           w_br_a, w_br_b, w_br_c, w_out):
    n_batch, seq, d = x.shape
    ctx_len = ctx.shape[1]
    depth = w_ada.shape[0]
    a_q_w, b_w, c_w = w_br_a.shape[1], w_br_b.shape[1], w_br_c.shape[1]
    a_kv_w = (w_in.shape[2] - a_q_w - 3 * b_w - 3 * c_w - 3 * d) // 2
    group = a_q_w // a_kv_w
    assert ctx_len == Q_TILE and seq % ROW_TILE == 0 and (n_batch * ctx_len) % ROW_TILE == 0
    assert seq % KEY_CHUNK == 0 and n_batch + 1 <= MOD_ROWS
    n_lat_rows = n_batch * seq
    n_ctx_rows = n_batch * ctx_len
    lat_tiles = n_lat_rows // ROW_TILE
    all_tiles = lat_tiles + n_ctx_rows // ROW_TILE
    n_all_rows = n_lat_rows + n_ctx_rows
    tiles_per_batch = seq // ROW_TILE
    seg_of_tile = lambda i: jnp.minimum(i // tiles_per_batch, n_batch)

    sizes = (a_q_w, a_kv_w, a_kv_w, b_w, b_w, b_w, c_w, c_w, c_w, 3 * d)
    offs = [int(o) for o in np.cumsum((0,) + sizes)]

    cvec = jnp.concatenate([c, c_ctx[None], jnp.zeros((MOD_ROWS - n_batch - 1, d), F32)], axis=0)
    mods_all = ada_modulation(cvec, w_ada, b_ada)

    rope_a = _rope_tables(seq, HEAD_DIM, n_batch, n_ctx_rows)
    rope_c = _rope_tables(seq, C_QK_DIM, n_batch, n_ctx_rows)
    rows = seq // GRID_W

    xs = (x.reshape(n_lat_rows, d), ctx.reshape(n_ctx_rows, d))

    w1i, w1o, w2i, w2o = ffn1_in, ffn1_out, ffn2_in, ffn2_out
    wp, wa, wb, wc, wo = w_in, w_br_a, w_br_b, w_br_c, w_out

    def gain_row(gain, width, scale):
        return jnp.tile(gain * scale, width // gain.shape[0]).reshape(1, width)

    sa, sc = HEAD_DIM ** -0.5 * LOG2E, C_QK_DIM ** -0.5 * LOG2E

    for l in range(depth):
        last = l == depth - 1
        with_ctx = not last
        lam_init = 0.8 - 0.6 * math.exp(-0.3 * l)
        mods = mods_all[l].reshape(MOD_ROWS, 1, N_MOD * d)

        h = norm_mod(xs, norm_g[l, 0], mods, 0, 1, seg_of_tile, all_tiles)
        a = ffn_up(h, w1i, l, n_all_rows)
        xs = resid_matmul(a, w1o, l, xs, mods, 2, 0.5, seg_of_tile, all_tiles, tn=512)

        h = norm_mod(xs, norm_g[l, 1], mods, 3, 4, seg_of_tile, all_tiles)
        qa = proj_qk(h, wp, l, offs[0], a_q_w, gain_row(qk_gain_a[l, 0], a_q_w, sa), HEAD_DIM, rope_a, n_all_rows)
        ka = proj_qk(h, wp, l, offs[1], a_kv_w, gain_row(qk_gain_a[l, 1], a_kv_w, 1.0), HEAD_DIM, rope_a, n_all_rows)
        va = proj_plain(h, wp, l, offs[2], a_kv_w, n_all_rows)
        qb = proj_qk(h, wp, l, offs[3], b_w, gain_row(qk_gain_b[l, 0], b_w, sa), HEAD_DIM, None, n_all_rows)
        kb = proj_qk(h, wp, l, offs[4], b_w, gain_row(qk_gain_b[l, 1], b_w, 1.0), HEAD_DIM, None, n_all_rows)
        vb = proj_plain(h, wp, l, offs[5], b_w, n_all_rows)
        qc = proj_qk(h, wp, l, offs[6], c_w, gain_row(qk_gain_c[l, 0], c_w, sc), C_QK_DIM, rope_c, n_all_rows)
        kc = proj_qk(h, wp, l, offs[7], c_w, gain_row(qk_gain_c[l, 1], c_w, 1.0), C_QK_DIM, rope_c, n_all_rows)
        vc = proj_plain(h, wp, l, offs[8], c_w, n_all_rows)
        mix_tiles = all_tiles if with_ctx else lat_tiles
        mix_rows = mix_tiles * ROW_TILE
        gates = proj_plain(h, wp, l, offs[9], 3 * d, mix_rows, tn=min(1024, d))

        o_a = gqa_attention(qa, ka, va, n_batch, seq, ctx_len, group, with_ctx)
        bias_tabs = na_bias_tables(na_rel_bias[l], rows)
        o_b = neighbourhood_attention(qb, kb, vb, bias_tabs, n_batch, seq, ctx_len, with_ctx)
        o_c = diff_attention(qc, kc, vc, diff_lambda[l], diff_subln[l], lam_init, n_batch, seq, ctx_len, with_ctx)

        m = gated_merge(o_a, o_b, o_c, wa, wb, wc, l, gates, mix_tiles)
        xs = resid_matmul(m, wo, l, xs, mods, 5, 1.0, seg_of_tile, mix_tiles, tn=1024)

        h = norm_mod(xs, norm_g[l, 2], mods, 6, 7, seg_of_tile, mix_tiles)
        a = ffn_up(h, w2i, l, mix_rows)
        xs = resid_matmul(a, w2o, l, xs, mods, 8, 0.5, seg_of_tile, mix_tiles, tn=512)

    return xs.reshape(n_batch, seq, d)
```

```python
import functools
import math
import typing

import numpy as np
import jax
import jax.numpy as jnp
from jax import lax
from jax.experimental import pallas as pl
from jax.experimental.pallas import tpu as pltpu

F32 = jnp.float32
BF16 = jnp.bfloat16

GRID_W = 64
HEAD_DIM = 128
C_QK_DIM = 64
NA_KH = 8
NA_KW = 16
ROPE_THETA = 10000.0
EPS = 1e-6
N_MOD = 9
LANES = 128
MOD_ROWS = 8
NEG_BIAS = -1e30
LOG2E = math.log2(math.e)
KEY_CHUNK = 512

V7X_VMEM_BYTES = 64 * 1024 * 1024
VMEM_LIMIT = (V7X_VMEM_BYTES * 7) // 8

ROW_TILE = 512
BF16_SUBLANES = 16
MAX_W_CHUNKS = 8
W_STAGE_SLOTS = 2
FFN_DOWN_ROWS = 256
QK_SUB_BLOCKS = 4
MAX_ROW_TILE = 1088
NA_HEADS = 4
DIFF_HEADS = 4
GQA_KV_HEADS = 4
Q_TILE = 256
NA_ROWS = Q_TILE // GRID_W


def _params(*sem):
    return pltpu.CompilerParams(dimension_semantics=sem, vmem_limit_bytes=VMEM_LIMIT)


def _sigmoid(v):
    return 1.0 / (1.0 + jnp.exp(-v))


def _dot(a, b):
    return jnp.dot(a, b, preferred_element_type=F32)


def _dot_t(a, b):
    return lax.dot_general(a, b, (((1,), (1,)), ((), ())), preferred_element_type=F32)


def _ada_kernel(c_ref, w_ref, b_ref, o_ref):
    c = c_ref[...]
    a = (c * _sigmoid(c)).astype(BF16)
    o_ref[0] = _dot(a, w_ref[0].astype(BF16)) + b_ref[0]


def ada_modulation(cvec, w_ada, b_ada, tn=512):
    depth, d, n = w_ada.shape
    return pl.pallas_call(
        _ada_kernel,
        grid=(depth, n // tn),
        in_specs=[
            pl.BlockSpec((MOD_ROWS, d), lambda l, j: (0, 0)),
            pl.BlockSpec((1, d, tn), lambda l, j: (l, 0, j)),
            pl.BlockSpec((1, 1, tn), lambda l, j: (l, 0, j)),
        ],
        out_specs=pl.BlockSpec((1, MOD_ROWS, tn), lambda l, j: (l, 0, j)),
        out_shape=jax.ShapeDtypeStruct((depth, MOD_ROWS, n), F32),
        compiler_params=_params("arbitrary", "arbitrary"),
        name="ada_mod",
    )(cvec, w_ada, b_ada.reshape(depth, 1, n))


def _row_sources(x, n_tiles, block_cols, col_of, tm=ROW_TILE):
    if not isinstance(x, tuple):
        return [x], [pl.BlockSpec((tm, block_cols), lambda *g: (g[-1], col_of(*g)))], None
    n_first = x[0].shape[0] // tm
    assert x[0].shape[0] % tm == 0 and x[1].shape[0] == (n_tiles - n_first) * tm
    specs = [pl.BlockSpec((tm, block_cols), lambda *g: (jnp.minimum(g[-1], n_first - 1), col_of(*g))),
             pl.BlockSpec((tm, block_cols), lambda *g: (jnp.maximum(g[-1] - n_first, 0), col_of(*g)))]
    return list(x), specs, n_first


def _pick_rows(x_refs, n_first, i):
    if n_first is None:
        return x_refs[0][...]
    return jnp.where(i < n_first, x_refs[0][...], x_refs[1][...])


def _normmod_kernel(*refs, n_first):
    *x_refs, g_ref, shift_ref, scale_ref, o_ref = refs
    x = _pick_rows(x_refs, n_first, pl.program_id(0))
    y = x * lax.rsqrt(jnp.mean(x * x, axis=-1, keepdims=True) + EPS) * g_ref[...]
    o_ref[...] = (y * (1.0 + scale_ref[0]) + shift_ref[0]).astype(o_ref.dtype)


def norm_mod(x, g, mods, shift_idx, scale_idx, seg_of_tile, n_tiles):
    d = g.shape[0]
    xs, x_specs, n_first = _row_sources(x, n_tiles, d, lambda i: 0)
    return pl.pallas_call(
        functools.partial(_normmod_kernel, n_first=n_first),
        grid=(n_tiles,),
        in_specs=x_specs + [
            pl.BlockSpec((1, d), lambda i: (0, 0)),
            pl.BlockSpec((1, 1, d), lambda i: (seg_of_tile(i), 0, shift_idx)),
            pl.BlockSpec((1, 1, d), lambda i: (seg_of_tile(i), 0, scale_idx)),
        ],
        out_specs=pl.BlockSpec((ROW_TILE, d), lambda i: (i, 0)),
        out_shape=jax.ShapeDtypeStruct((n_tiles * ROW_TILE, d), BF16),
        compiler_params=_params("arbitrary"),
        name="norm_mod",
    )(*xs, g.reshape(1, d), mods, mods)


class _WStream(typing.NamedTuple):
    hbm: typing.Any
    layer: int
    col0: int
    wbuf: typing.Any
    stage: typing.Any
    sem: typing.Any


def _n_chunks(k, n_row_tiles):
    for nk in range(min(MAX_W_CHUNKS, W_STAGE_SLOTS * (n_row_tiles - 1)), 0, -1):
        if k % nk == 0 and (k // nk) % BF16_SUBLANES == 0:
            return nk
    raise ValueError(f"no weight chunking for K={k} with {n_row_tiles} row tiles")


def _big_row_tile(n_rows):
    for tm in range(MAX_ROW_TILE, 0, -QK_SUB_BLOCKS * BF16_SUBLANES):
        if n_rows % tm == 0:
            return tm
    raise ValueError(f"no row tile for {n_rows} rows")


def _w_copy(st, tile, chunk, slot):
    _, ck, tn = st.stage.shape
    rows = pl.ds(pl.multiple_of(chunk * ck, BF16_SUBLANES), ck)
    cols = pl.ds(pl.multiple_of(st.col0 + tile * tn, LANES), tn)
    return pltpu.make_async_copy(st.hbm.at[st.layer, rows, cols], st.stage.at[slot], st.sem.at[slot])


def _w_round(st, buf, chunk, slot):
    ck = st.stage.shape[1]
    rows = pl.ds(pl.multiple_of(chunk * ck, BF16_SUBLANES), ck)
    st.wbuf[buf, rows, :] = st.stage[slot].astype(st.wbuf.dtype)


def _weight_pipeline(streams, nj, ni):
    j, i = pl.program_id(0), pl.program_id(1)
    nk = streams[0].wbuf.shape[1] // streams[0].stage.shape[1]
    cps = max(1, -(-nk // max(ni - 1, 1)))
    assert cps <= W_STAGE_SLOTS and (nj == 1 or (ni - 1) * cps >= nk)

    @pl.when((j == 0) & (i == 0))
    def _prime():
        for st in streams:
            _w_copy(st, 0, 0, 0).start()
        for c in range(nk):
            for st in streams:
                if c + 1 < nk:
                    _w_copy(st, 0, c + 1, (c + 1) % W_STAGE_SLOTS).start()
                _w_copy(st, 0, c, c % W_STAGE_SLOTS).wait()
                _w_round(st, 0, c, c % W_STAGE_SLOTS)

    @pl.when(j + 1 < nj)
    def _prefetch():
        for s in range(cps):
            prev = (i - 1) * cps + s

            @pl.when((i >= 1) & (prev < nk))
            def _(prev=prev, s=s):
                for st in streams:
                    _w_copy(st, j + 1, prev, s).wait()
                    _w_round(st, (j + 1) % 2, prev, s)

        for s in range(cps):
            cur = i * cps + s

            @pl.when(cur < nk)
            def _(cur=cur, s=s):
                for st in streams:
                    _w_copy(st, j + 1, cur, s).start()

    return j % 2


def _w_scratch(k, tn, nk):
    return [pltpu.VMEM((2, k, tn), BF16), pltpu.VMEM((W_STAGE_SLOTS, k // nk, tn), F32),
            pltpu.SemaphoreType.DMA((W_STAGE_SLOTS,))]


_HBM = pl.BlockSpec(memory_space=pl.ANY)


def _row_halves(ref, n=2):
    sub = ref.shape[0] // n
    assert sub * n == ref.shape[0] and sub % BF16_SUBLANES == 0
    return tuple(slice(r * sub, (r + 1) * sub) for r in range(n))


def _swiglu_kernel(a_ref, w_hbm, o_ref, wg, sg, semg, wu, su, semu, *, layer, nj, ni):
    tn = wg.shape[2]
    buf = _weight_pipeline([_WStream(w_hbm, layer, 0, wg, sg, semg),
                            _WStream(w_hbm, layer, nj * tn, wu, su, semu)], nj, ni)
    for rows in _row_halves(a_ref):
        a = a_ref[rows, :]
        g = _dot(a, wg[buf])
        u = _dot(a, wu[buf])
        o_ref[rows, :] = (g * _sigmoid(g) * u).astype(o_ref.dtype)


def ffn_up(h, w_in, layer, n_rows, tn=512):
    k = h.shape[1]
    f = w_in.shape[2] // 2
    nj = f // tn
    tm = _big_row_tile(n_rows)
    ni = n_rows // tm
    nk = _n_chunks(k, ni)
    return pl.pallas_call(
        functools.partial(_swiglu_kernel, layer=layer, nj=nj, ni=ni),
        grid=(nj, ni),
        in_specs=[pl.BlockSpec((tm, k), lambda j, i: (i, 0)), _HBM],
        out_specs=pl.BlockSpec((tm, tn), lambda j, i: (i, j)),
        out_shape=jax.ShapeDtypeStruct((n_rows, f), BF16),
        scratch_shapes=_w_scratch(k, tn, nk) + _w_scratch(k, tn, nk),
        compiler_params=_params("arbitrary", "arbitrary"),
        name="ffn_up",
    )(h, w_in)


def _resid_kernel(a_ref, w_hbm, *refs, layer, nj, ni, coef, n_first):
    *x_refs, gate_ref, o_ref, wb, sb, semb = refs
    buf = _weight_pipeline([_WStream(w_hbm, layer, 0, wb, sb, semb)], nj, ni)
    y = _dot(a_ref[...], wb[buf])
    o_ref[...] = _pick_rows(x_refs, n_first, pl.program_id(1)) + (coef * gate_ref[0]) * y


def resid_matmul(a, w, layer, x, mods, gate_idx, coef, seg_of_tile, n_tiles, tn, tm=ROW_TILE):
    _, k, n = w.shape
    tn = min(tn, n)
    nj = n // tn
    assert ROW_TILE % tm == 0
    ni = n_tiles * (ROW_TILE // tm)
    nk = _n_chunks(k, ni)
    xs, x_specs, n_first = _row_sources(x, ni, tn, lambda j, i: j, tm)
    return pl.pallas_call(
        functools.partial(_resid_kernel, layer=layer, nj=nj, ni=ni, coef=coef, n_first=n_first),
        grid=(nj, ni),
        in_specs=[pl.BlockSpec((tm, k), lambda j, i: (i, 0)), _HBM] + x_specs + [
            pl.BlockSpec((1, 1, tn), lambda j, i: (seg_of_tile(i * tm // ROW_TILE), 0, gate_idx * nj + j)),
        ],
        out_specs=pl.BlockSpec((tm, tn), lambda j, i: (i, j)),
        out_shape=jax.ShapeDtypeStruct((n_tiles * ROW_TILE, n), F32),
        scratch_shapes=_w_scratch(k, tn, nk),
        compiler_params=_params("arbitrary", "arbitrary"),
        name="resid_matmul",
    )(a, w, *xs, mods)


def _proj_kernel(a_ref, w_hbm, o_ref, wb, sb, semb, *, layer, col0, nj, ni):
    buf = _weight_pipeline([_WStream(w_hbm, layer, col0, wb, sb, semb)], nj, ni)
    for rows in _row_halves(a_ref):
        o_ref[rows, :] = _dot(a_ref[rows, :], wb[buf]).astype(o_ref.dtype)


def proj_plain(h, w, layer, col0, width, n_rows, tn=512):
    k = h.shape[1]
    nj = width // tn
    tm = _big_row_tile(n_rows)
    ni = n_rows // tm
    nk = _n_chunks(k, ni)
    return pl.pallas_call(
        functools.partial(_proj_kernel, layer=layer, col0=col0, nj=nj, ni=ni),
        grid=(nj, ni),
        in_specs=[pl.BlockSpec((tm, k), lambda j, i: (i, 0)), _HBM],
        out_specs=pl.BlockSpec((tm, tn), lambda j, i: (i, j)),
        out_shape=jax.ShapeDtypeStruct((n_rows, width), BF16),
        scratch_shapes=_w_scratch(k, tn, nk),
        compiler_params=_params("arbitrary", "arbitrary"),
        name="proj_plain",
    )(h, w)


def _swap_pairs(y):
    lane = lax.broadcasted_iota(jnp.int32, y.shape, 1)
    nxt = pltpu.roll(y, LANES - 1, 1)
    prv = pltpu.roll(y, 1, 1)
    return jnp.where(lane % 2 == 0, nxt, prv)


def _proj_qk_kernel(a_ref, w_hbm, gain_ref, *rest, layer, col0, nj, ni, head_dim, rope):
    if rope:
        cos_ref, sin_ref, o_ref, wb, sb, semb = rest
    else:
        o_ref, wb, sb, semb = rest
    buf = _weight_pipeline([_WStream(w_hbm, layer, col0, wb, sb, semb)], nj, ni)
    tn = wb.shape[2]
    for rows in _row_halves(a_ref, QK_SUB_BLOCKS):
        acc = _dot(a_ref[rows, :], wb[buf])
        for s in range(tn // LANES):
            sl = slice(s * LANES, (s + 1) * LANES)
            y = acc[:, sl]
            sq = y * y
            if head_dim == LANES:
                ms = jnp.mean(sq, axis=-1, keepdims=True)
            else:
                lane = lax.broadcasted_iota(jnp.int32, y.shape, 1)
                low = lane < head_dim
                s_low = jnp.sum(jnp.where(low, sq, 0.0), axis=-1, keepdims=True)
                s_high = jnp.sum(jnp.where(low, 0.0, sq), axis=-1, keepdims=True)
                ms = jnp.where(low, s_low, s_high) * (1.0 / head_dim)
            y = y * lax.rsqrt(ms + EPS) * gain_ref[:, sl]
            if rope:
                y = y * cos_ref[rows, :] + _swap_pairs(y) * sin_ref[rows, :]
            o_ref[rows, sl] = y.astype(o_ref.dtype)


def proj_qk(h, w, layer, col0, width, gain_row, head_dim, rope_tabs, n_rows, tn=512):
    k = h.shape[1]
    nj = width // tn
    tm = _big_row_tile(n_rows)
    ni = n_rows // tm
    nk = _n_chunks(k, ni)
    rope = rope_tabs is not None
    in_specs = [
        pl.BlockSpec((tm, k), lambda j, i: (i, 0)),
        _HBM,
        pl.BlockSpec((1, tn), lambda j, i: (0, j)),
    ]
    args = [h, w, gain_row]
    if rope:
        in_specs += [pl.BlockSpec((tm, LANES), lambda j, i: (i, 0))] * 2
        args += list(rope_tabs)
    return pl.pallas_call(
        functools.partial(_proj_qk_kernel, layer=layer, col0=col0, nj=nj, ni=ni, head_dim=head_dim, rope=rope),
        grid=(nj, ni),
        in_specs=in_specs,
        out_specs=pl.BlockSpec((tm, tn), lambda j, i: (i, j)),
        out_shape=jax.ShapeDtypeStruct((n_rows, width), BF16),
        scratch_shapes=_w_scratch(k, tn, nk),
        compiler_params=_params("arbitrary", "arbitrary"),
        name="proj_qk",
    )(*args)


def _merge_kernel(oa_ref, ob_ref, oc_ref, wa_hbm, wb_hbm, wc_hbm, ga_ref, gb_ref, gc_ref, o_ref,
                  wa, sa, sema, wb, sb, semb, wc, sc, semc, *, layer, nj, ni):
    buf = _weight_pipeline([_WStream(wa_hbm, layer, 0, wa, sa, sema), _WStream(wb_hbm, layer, 0, wb, sb, semb),
                            _WStream(wc_hbm, layer, 0, wc, sc, semc)], nj, ni)
    m = _sigmoid(ga_ref[...].astype(F32)) * _dot(oa_ref[...], wa[buf])
    m += _sigmoid(gb_ref[...].astype(F32)) * _dot(ob_ref[...], wb[buf])
    m += _sigmoid(gc_ref[...].astype(F32)) * _dot(oc_ref[...], wc[buf])
    o_ref[...] = m.astype(o_ref.dtype)


def gated_merge(oa, ob, oc, wa, wb, wc, layer, gates, n_tiles, tn=1024):
    d = wa.shape[2]
    tn = min(tn, d)
    nj = d // tn
    nk = min(_n_chunks(w.shape[1], n_tiles) for w in (wa, wb, wc))
    assert all(w.shape[1] % nk == 0 and (w.shape[1] // nk) % BF16_SUBLANES == 0 for w in (wa, wb, wc))
    act = lambda o: pl.BlockSpec((ROW_TILE, o.shape[1]), lambda j, i: (i, 0))
    gate = lambda b: pl.BlockSpec((ROW_TILE, tn), lambda j, i: (i, j + b * nj))
    return pl.pallas_call(
        functools.partial(_merge_kernel, layer=layer, nj=nj, ni=n_tiles),
        grid=(nj, n_tiles),
        in_specs=[act(oa), act(ob), act(oc), _HBM, _HBM, _HBM, gate(0), gate(1), gate(2)],
        out_specs=pl.BlockSpec((ROW_TILE, tn), lambda j, i: (i, j)),
        out_shape=jax.ShapeDtypeStruct((n_tiles * ROW_TILE, d), BF16),
        scratch_shapes=sum((_w_scratch(w.shape[1], tn, nk) for w in (wa, wb, wc)), []),
        compiler_params=_params("arbitrary", "arbitrary"),
        name="gated_merge",
    )(oa, ob, oc, wa, wb, wc, gates, gates, gates)


def _head_slices(n):
    return [slice(j * HEAD_DIM, (j + 1) * HEAD_DIM) for j in range(n)]


def _build_vaug(vaug_ref, vl_ref, vc_ref):
    seq = vl_ref.shape[0]
    for j, hs in enumerate(_head_slices(vaug_ref.shape[0])):
        vaug_ref[j, 0:seq, 0:HEAD_DIM] = vl_ref[:, hs]
        vaug_ref[j, seq:, 0:HEAD_DIM] = vc_ref[:, hs]
        vaug_ref[j, :, HEAD_DIM:] = jnp.ones((vaug_ref.shape[1], HEAD_DIM), vaug_ref.dtype)


def _attend(q, parts):
    acc = m_run = None
    for k, vaug, bias in parts:
        s = _dot_t(q, k)
        if bias is not None:
            s = s + bias
        m_new = jnp.max(s, axis=-1, keepdims=True)
        if acc is not None:
            m_new = jnp.maximum(m_run, m_new)
        o = _dot(jnp.exp2(s - m_new).astype(vaug.dtype), vaug)
        acc = o if acc is None else jnp.exp2(m_run - m_new) * acc + o
        m_run = m_new
    return acc[:, :HEAD_DIM] / acc[:, HEAD_DIM:]


def _key_parts(kl_ref, kc_ref, vaug_ref, latent, j, hs):
    seq = kl_ref.shape[0]
    parts = []
    if latent:
        for c in range(seq // KEY_CHUNK):
            sl = slice(c * KEY_CHUNK, (c + 1) * KEY_CHUNK)
            parts.append((kl_ref[sl, hs], vaug_ref[j, sl, :], None))
    parts.append((kc_ref[:, hs], vaug_ref[j, seq:, :], None))
    return parts


def _gqa_kernel(q_ref, kl_ref, kc_ref, vl_ref, vc_ref, o_ref, vaug_ref, *, group, n_lat, with_ctx):
    qt = pl.program_id(2)
    pl.when(qt == 0)(lambda: _build_vaug(vaug_ref, vl_ref, vc_ref))

    def run(latent):
        for j, hs in enumerate(_head_slices(vaug_ref.shape[0])):
            cols = [slice((j * group + g) * HEAD_DIM, (j * group + g + 1) * HEAD_DIM) for g in range(group)]
            q = jnp.concatenate([q_ref[:, c] for c in cols], axis=0)
            o = _attend(q, _key_parts(kl_ref, kc_ref, vaug_ref, latent, j, hs))
            for g, c in enumerate(cols):
                o_ref[:, c] = o[g * Q_TILE:(g + 1) * Q_TILE].astype(o_ref.dtype)

    if with_ctx:
        pl.when(qt < n_lat)(lambda: run(True))
        pl.when(qt >= n_lat)(lambda: run(False))
    else:
        run(True)


def _q_row_block(b, qt, n_lat, n_batch):
    return jnp.where(qt < n_lat, b * n_lat + qt, n_batch * n_lat + b)


def gqa_attention(q, k, v, n_batch, seq, ctx_len, group, with_ctx):
    rows = q.shape[0] if with_ctx else n_batch * seq
    kvh = k.shape[1] // HEAD_DIM
    n_lat = seq // Q_TILE
    ctx_blk0 = n_batch * seq // ctx_len
    nh = GQA_KV_HEADS if kvh % GQA_KV_HEADS == 0 else 1
    gw, kw = nh * group * HEAD_DIM, nh * HEAD_DIM
    qmap = lambda b, h, t: (_q_row_block(b, t, n_lat, n_batch), h)
    lat = pl.BlockSpec((seq, kw), lambda b, h, t: (b, h))
    ctx = pl.BlockSpec((ctx_len, kw), lambda b, h, t: (ctx_blk0 + b, h))
    return pl.pallas_call(
        functools.partial(_gqa_kernel, group=group, n_lat=n_lat, with_ctx=with_ctx),
        grid=(n_batch, kvh // nh, n_lat + int(with_ctx)),
        in_specs=[pl.BlockSpec((Q_TILE, gw), qmap), lat, ctx, lat, ctx],
        out_specs=pl.BlockSpec((Q_TILE, gw), qmap),
        out_shape=jax.ShapeDtypeStruct((rows, q.shape[1]), BF16),
        scratch_shapes=[pltpu.VMEM((nh, seq + ctx_len, 2 * HEAD_DIM), BF16)],
        compiler_params=_params("arbitrary", "arbitrary", "arbitrary"),
        name="gqa_attention",
    )(q, k, k, v, v)


def _diff_kernel(q_ref, kl_ref, kc_ref, vl_ref, vc_ref, lam_ref, g_ref, o_ref, vaug_ref, *,
                 lam_init, n_lat, with_ctx):
    qt = pl.program_id(2)
    pl.when(qt == 0)(lambda: _build_vaug(vaug_ref, vl_ref, vc_ref))
    lp = lam_ref[...]
    lam = (jnp.exp(jnp.sum(lp[0:1] * lp[1:2], axis=-1, keepdims=True))
           - jnp.exp(jnp.sum(lp[2:3] * lp[3:4], axis=-1, keepdims=True)) + lam_init)

    def run(latent):
        for j, hs in enumerate(_head_slices(vaug_ref.shape[0])):
            q = q_ref[:, hs]
            lane = lax.broadcasted_iota(jnp.int32, q.shape, 1)
            zero = jnp.zeros_like(q)
            q2 = jnp.concatenate([jnp.where(lane < C_QK_DIM, q, zero), jnp.where(lane < C_QK_DIM, zero, q)],
                                 axis=0)
            o2 = _attend(q2, _key_parts(kl_ref, kc_ref, vaug_ref, latent, j, hs))
            o = o2[:Q_TILE] - lam * o2[Q_TILE:]
            o = o * lax.rsqrt(jnp.mean(o * o, axis=-1, keepdims=True) + EPS) * g_ref[...]
            o_ref[:, hs] = (o * (1.0 - lam_init)).astype(o_ref.dtype)

    if with_ctx:
        pl.when(qt < n_lat)(lambda: run(True))
        pl.when(qt >= n_lat)(lambda: run(False))
    else:
        run(True)


def diff_attention(q, k, v, lam_params, subln, lam_init, n_batch, seq, ctx_len, with_ctx):
    rows = q.shape[0] if with_ctx else n_batch * seq
    n_heads = v.shape[1] // HEAD_DIM
    n_lat = seq // Q_TILE
    ctx_blk0 = n_batch * seq // ctx_len
    nh = DIFF_HEADS if n_heads % DIFF_HEADS == 0 else 1
    hw = nh * HEAD_DIM
    qmap = lambda b, h, t: (_q_row_block(b, t, n_lat, n_batch), h)
    lat = pl.BlockSpec((seq, hw), lambda b, h, t: (b, h))
    ctx = pl.BlockSpec((ctx_len, hw), lambda b, h, t: (ctx_blk0 + b, h))
    full = lambda a: pl.BlockSpec(a.shape, lambda b, h, t: (0,) * a.ndim)
    subln = subln.reshape(1, -1)
    return pl.pallas_call(
        functools.partial(_diff_kernel, lam_init=lam_init, n_lat=n_lat, with_ctx=with_ctx),
        grid=(n_batch, n_heads // nh, n_lat + int(with_ctx)),
        in_specs=[pl.BlockSpec((Q_TILE, hw), qmap), lat, ctx, lat, ctx, full(lam_params), full(subln)],
        out_specs=pl.BlockSpec((Q_TILE, hw), qmap),
        out_shape=jax.ShapeDtypeStruct((rows, v.shape[1]), BF16),
        scratch_shapes=[pltpu.VMEM((nh, seq + ctx_len, 2 * HEAD_DIM), BF16)],
        compiler_params=_params("arbitrary", "arbitrary", "arbitrary"),
        name="diff_attention",
    )(q, k, k, v, v, lam_params, subln)


def _na_plan(rows):
    kh = min(NA_KH, rows)
    band = min(NA_ROWS + kh, rows)
    patterns, types = [], []
    for blk in range(rows // NA_ROWS):
        r0 = blk * NA_ROWS
        bs = min(max(r0 - kh // 2, 0), rows - band)
        pat = []
        for qr in range(r0, r0 + NA_ROWS):
            rs = min(max(qr - kh // 2, 0), rows - kh)
            pat.append(tuple((kr - qr + kh - 1) if rs <= kr < rs + kh else -1 for kr in range(bs, bs + band)))
        pat = tuple(pat)
        if pat not in patterns:
            patterns.append(pat)
        types.append(patterns.index(pat))
    return band, np.asarray(types, np.int32), patterns


def _na_bias_kernel(rb_ref, o_ref, *, kh, patterns):
    h = pl.program_id(0)
    n_dc = 2 * NA_KW - 1
    base = h * ((2 * NA_KH - 1) * n_dc)
    qc = lax.broadcasted_iota(jnp.int32, (GRID_W, GRID_W), 0)
    kc = lax.broadcasted_iota(jnp.int32, (GRID_W, GRID_W), 1)
    dc = kc - qc
    cs = jnp.clip(qc - NA_KW // 2, 0, GRID_W - NA_KW)
    col_ok = (kc >= cs) & (kc < cs + NA_KW)
    neg = jnp.full((GRID_W, GRID_W), NEG_BIAS, F32)
    used = sorted({a for pat in patterns for row in pat for a in row if a >= 0})
    toep = {}
    for a in used:
        a_full = a + (NA_KH - kh)
        t = neg
        for b in range(n_dc):
            t = jnp.where(dc == b - (NA_KW - 1), rb_ref[base + a_full * n_dc + b] * LOG2E, t)
        toep[a] = jnp.where(col_ok, t, neg)
    for t_id, pat in enumerate(patterns):
        for qr, row in enumerate(pat):
            blocks = [toep[a] if a >= 0 else neg for a in row]
            o_ref[0, t_id, qr * GRID_W:(qr + 1) * GRID_W, :] = jnp.concatenate(blocks, axis=1)


def na_bias_tables(rel_bias, rows):
    n_heads = rel_bias.shape[0]
    kh = min(NA_KH, rows)
    band, _, patterns = _na_plan(rows)
    shape = (n_heads, len(patterns), Q_TILE, band * GRID_W)
    return pl.pallas_call(
        functools.partial(_na_bias_kernel, kh=kh, patterns=patterns),
        grid=(n_heads,),
        in_specs=[pl.BlockSpec(memory_space=pltpu.SMEM)],
        out_specs=pl.BlockSpec((1,) + shape[1:], lambda h: (h, 0, 0, 0)),
        out_shape=jax.ShapeDtypeStruct(shape, F32),
        compiler_params=_params("arbitrary"),
        name="na_bias_tables",
    )(rel_bias.reshape(-1))


def _na_kernel(types_ref, q_ref, kl_ref, kc_ref, vl_ref, vc_ref, bias_ref, o_ref, vaug_ref, *,
               rows, band, n_lat, with_ctx):
    del types_ref
    rt = pl.program_id(2)
    kh = min(NA_KH, rows)
    seq = kl_ref.shape[0]
    heads = _head_slices(NA_HEADS)
    pl.when(rt == 0)(lambda: _build_vaug(vaug_ref, vl_ref, vc_ref))
    ctx_part = lambda j, hs: (kc_ref[:, hs], vaug_ref[j, seq:, :], None)

    def latent():
        bs = jnp.clip(rt * NA_ROWS - kh // 2, 0, rows - band)
        band_sl = pl.ds(pl.multiple_of(bs * GRID_W, GRID_W), band * GRID_W)
        for j, hs in enumerate(heads):
            parts = [(kl_ref[band_sl, hs], vaug_ref[j, band_sl, :], bias_ref[j, 0]), ctx_part(j, hs)]
            o_ref[:, hs] = _attend(q_ref[:, hs], parts).astype(o_ref.dtype)

    def context():
        for j, hs in enumerate(heads):
            o_ref[:, hs] = _attend(q_ref[:, hs], [ctx_part(j, hs)]).astype(o_ref.dtype)

    if with_ctx:
        pl.when(rt < n_lat)(latent)
        pl.when(rt >= n_lat)(context)
    else:
        latent()


def neighbourhood_attention(q, k, v, bias_tabs, n_batch, seq, ctx_len, with_ctx):
    rows_tok = q.shape[0] if with_ctx else n_batch * seq
    n_heads = q.shape[1] // HEAD_DIM
    rows = seq // GRID_W
    band, types, _ = _na_plan(rows)
    n_lat = seq // Q_TILE
    ctx_blk0 = n_batch * seq // ctx_len
    types = jnp.asarray(np.concatenate([types, types[-1:]]))
    hw = NA_HEADS * HEAD_DIM
    assert n_heads % NA_HEADS == 0
    qmap = lambda b, h, t, ty: (_q_row_block(b, t, n_lat, n_batch), h)
    lat = pl.BlockSpec((seq, hw), lambda b, h, t, ty: (b, h))
    ctx = pl.BlockSpec((ctx_len, hw), lambda b, h, t, ty: (ctx_blk0 + b, h))
    bias = pl.BlockSpec((NA_HEADS, 1, Q_TILE, band * GRID_W), lambda b, h, t, ty: (h, ty[t], 0, 0))
    return pl.pallas_call(
        functools.partial(_na_kernel, rows=rows, band=band, n_lat=n_lat, with_ctx=with_ctx),
        grid_spec=pltpu.PrefetchScalarGridSpec(
            num_scalar_prefetch=1,
            grid=(n_batch, n_heads // NA_HEADS, n_lat + int(with_ctx)),
            in_specs=[pl.BlockSpec((Q_TILE, hw), qmap), lat, ctx, lat, ctx, bias],
            out_specs=pl.BlockSpec((Q_TILE, hw), qmap),
            scratch_shapes=[pltpu.VMEM((NA_HEADS, seq + ctx_len, 2 * HEAD_DIM), BF16)],
        ),
        out_shape=jax.ShapeDtypeStruct((rows_tok, q.shape[1]), BF16),
        compiler_params=_params("arbitrary", "arbitrary", "arbitrary"),
        name="neighbourhood_attention",
    )(types, q, k, k, v, v, bias_tabs)


def _rope_tables(seq, dim, n_batch, n_ctx_rows):
    t = jnp.arange(seq, dtype=jnp.int32)
    row = (t // GRID_W).astype(F32)
    col = (t % GRID_W).astype(F32)
    n_pairs = dim // 4
    inv = ROPE_THETA ** (-jnp.arange(n_pairs, dtype=F32) / n_pairs)
    ang = jnp.concatenate([row[:, None] * inv, col[:, None] * inv], axis=-1)
    cos = jnp.repeat(jnp.cos(ang), 2, axis=-1)
    sin = jnp.stack([-jnp.sin(ang), jnp.sin(ang)], axis=-1).reshape(seq, dim)
    reps = LANES // dim
    cos, sin = jnp.tile(cos, (n_batch, reps)), jnp.tile(sin, (n_batch, reps))
    cos = jnp.concatenate([cos, jnp.ones((n_ctx_rows, LANES), F32)], axis=0)
    sin = jnp.concatenate([sin, jnp.zeros((n_ctx_rows, LANES), F32)], axis=0)
    return cos, sin


def kernel(x, c, ctx, c_ctx, w_ada, b_ada, norm_g, ffn1_in, ffn1_out, ffn2_in, ffn2_out, w_in,
           qk_gain_a, qk_gain_b, qk_gain_c, na_rel_bias, diff_lambda, diff_subln,
           w_br_a, w_br_b, w_br_c, w_out):
    n_batch, seq, d = x.shape
    ctx_len = ctx.shape[1]
    depth = w_ada.shape[0]
    a_q_w, b_w, c_w = w_br_a.shape[1], w_br_b.shape[1], w_br_c.shape[1]
    a_kv_w = (w_in.shape[2] - a_q_w - 3 * b_w - 3 * c_w - 3 * d) // 2
    group = a_q_w // a_kv_w
    assert ctx_len == Q_TILE and seq % ROW_TILE == 0 and (n_batch * ctx_len) % ROW_TILE == 0
    assert seq % KEY_CHUNK == 0 and n_batch + 1 <= MOD_ROWS
    n_lat_rows = n_batch * seq
    n_ctx_rows = n_batch * ctx_len
    lat_tiles = n_lat_rows // ROW_TILE
    all_tiles = lat_tiles + n_ctx_rows // ROW_TILE
    n_all_rows = n_lat_rows + n_ctx_rows
    tiles_per_batch = seq // ROW_TILE
    seg_of_tile = lambda i: jnp.minimum(i // tiles_per_batch, n_batch)

    sizes = (a_q_w, a_kv_w, a_kv_w, b_w, b_w, b_w, c_w, c_w, c_w, 3 * d)
    offs = [int(o) for o in np.cumsum((0,) + sizes)]

    cvec = jnp.concatenate([c, c_ctx[None], jnp.zeros((MOD_ROWS - n_batch - 1, d), F32)], axis=0)
    mods_all = ada_modulation(cvec, w_ada, b_ada)

    rope_a = _rope_tables(seq, HEAD_DIM, n_batch, n_ctx_rows)
    rope_c = _rope_tables(seq, C_QK_DIM, n_batch, n_ctx_rows)
    rows = seq // GRID_W

    xs = (x.reshape(n_lat_rows, d), ctx.reshape(n_ctx_rows, d))

    w1i, w1o, w2i, w2o = ffn1_in, ffn1_out, ffn2_in, ffn2_out
    wp, wa, wb, wc, wo = w_in, w_br_a, w_br_b, w_br_c, w_out

    def gain_row(gain, width, scale):
        return jnp.tile(gain * scale, width // gain.shape[0]).reshape(1, width)

    sa, sc = HEAD_DIM ** -0.5 * LOG2E, C_QK_DIM ** -0.5 * LOG2E

    for l in range(depth):
        last = l == depth - 1
        with_ctx = not last
        lam_init = 0.8 - 0.6 * math.exp(-0.3 * l)
        mods = mods_all[l].reshape(MOD_ROWS, 1, N_MOD * d)

        h = norm_mod(xs, norm_g[l, 0], mods, 0, 1, seg_of_tile, all_tiles)
        a = ffn_up(h, w1i, l, n_all_rows)
        xs = resid_matmul(a, w1o, l, xs, mods, 2, 0.5, seg_of_tile, all_tiles, tn=1024, tm=FFN_DOWN_ROWS)

        h = norm_mod(xs, norm_g[l, 1], mods, 3, 4, seg_of_tile, all_tiles)
        qa = proj_qk(h, wp, l, offs[0], a_q_w, gain_row(qk_gain_a[l, 0], a_q_w, sa), HEAD_DIM, rope_a, n_all_rows)
        ka = proj_qk(h, wp, l, offs[1], a_kv_w, gain_row(qk_gain_a[l, 1], a_kv_w, 1.0), HEAD_DIM, rope_a, n_all_rows)
        va = proj_plain(h, wp, l, offs[2], a_kv_w, n_all_rows)
        qb = proj_qk(h, wp, l, offs[3], b_w, gain_row(qk_gain_b[l, 0], b_w, sa), HEAD_DIM, None, n_all_rows)
        kb = proj_qk(h, wp, l, offs[4], b_w, gain_row(qk_gain_b[l, 1], b_w, 1.0), HEAD_DIM, None, n_all_rows)
        vb = proj_plain(h, wp, l, offs[5], b_w, n_all_rows)
        qc = proj_qk(h, wp, l, offs[6], c_w, gain_row(qk_gain_c[l, 0], c_w, sc), C_QK_DIM, rope_c, n_all_rows)
        kc = proj_qk(h, wp, l, offs[7], c_w, gain_row(qk_gain_c[l, 1], c_w, 1.0), C_QK_DIM, rope_c, n_all_rows)
        vc = proj_plain(h, wp, l, offs[8], c_w, n_all_rows)
        mix_tiles = all_tiles if with_ctx else lat_tiles
        mix_rows = mix_tiles * ROW_TILE
        gates = proj_plain(h, wp, l, offs[9], 3 * d, mix_rows, tn=min(1024, d))

        o_a = gqa_attention(qa, ka, va, n_batch, seq, ctx_len, group, with_ctx)
        bias_tabs = na_bias_tables(na_rel_bias[l], rows)
        o_b = neighbourhood_attention(qb, kb, vb, bias_tabs, n_batch, seq, ctx_len, with_ctx)
        o_c = diff_attention(qc, kc, vc, diff_lambda[l], diff_subln[l], lam_init, n_batch, seq, ctx_len, with_ctx)

        m = gated_merge(o_a, o_b, o_c, wa, wb, wc, l, gates, mix_tiles)
        xs = resid_matmul(m, wo, l, xs, mods, 5, 1.0, seg_of_tile, mix_tiles, tn=1024)

        h = norm_mod(xs, norm_g[l, 2], mods, 6, 7, seg_of_tile, mix_tiles)
        a = ffn_up(h, w2i, l, mix_rows)
        xs = resid_matmul(a, w2o, l, xs, mods, 8, 0.5, seg_of_tile, mix_tiles, tn=1024, tm=FFN_DOWN_ROWS)

    return xs.reshape(n_batch, seq, d)
```

```python
import functools
import math
import typing

import numpy as np
import jax
import jax.numpy as jnp
from jax import lax
from jax.experimental import pallas as pl
from jax.experimental.pallas import tpu as pltpu

F32 = jnp.float32
BF16 = jnp.bfloat16

GRID_W = 64
HEAD_DIM = 128
C_QK_DIM = 64
NA_KH = 8
NA_KW = 16
ROPE_THETA = 10000.0
EPS = 1e-6
N_MOD = 9
LANES = 128
MOD_ROWS = 8
NEG_BIAS = -1e30
LOG2E = math.log2(math.e)
KEY_CHUNK = 512

V7X_VMEM_BYTES = 64 * 1024 * 1024
VMEM_LIMIT = (V7X_VMEM_BYTES * 7) // 8

ROW_TILE = 512
BF16_SUBLANES = 16
MAX_W_CHUNKS = 8
W_STAGE_SLOTS = 2
FFN_DOWN_ROWS = 256
QK_SUB_BLOCKS = 4
MAX_ROW_TILE = 1088
NA_HEADS = 6
DIFF_HEADS = 4
GQA_KV_HEADS = 4
Q_TILE = 256
NA_ROWS = Q_TILE // GRID_W


def _params(*sem):
    return pltpu.CompilerParams(dimension_semantics=sem, vmem_limit_bytes=VMEM_LIMIT)


def _sigmoid(v):
    return 1.0 / (1.0 + jnp.exp(-v))


def _dot(a, b):
    return jnp.dot(a, b, preferred_element_type=F32)


def _dot_t(a, b):
    return lax.dot_general(a, b, (((1,), (1,)), ((), ())), preferred_element_type=F32)


def _ada_kernel(c_ref, w_ref, b_ref, o_ref):
    c = c_ref[...]
    a = (c * _sigmoid(c)).astype(BF16)
    o_ref[0] = _dot(a, w_ref[0].astype(BF16)) + b_ref[0]


def ada_modulation(cvec, w_ada, b_ada, tn=512):
    depth, d, n = w_ada.shape
    return pl.pallas_call(
        _ada_kernel,
        grid=(depth, n // tn),
        in_specs=[
            pl.BlockSpec((MOD_ROWS, d), lambda l, j: (0, 0)),
            pl.BlockSpec((1, d, tn), lambda l, j: (l, 0, j)),
            pl.BlockSpec((1, 1, tn), lambda l, j: (l, 0, j)),
        ],
        out_specs=pl.BlockSpec((1, MOD_ROWS, tn), lambda l, j: (l, 0, j)),
        out_shape=jax.ShapeDtypeStruct((depth, MOD_ROWS, n), F32),
        compiler_params=_params("arbitrary", "arbitrary"),
        name="ada_mod",
    )(cvec, w_ada, b_ada.reshape(depth, 1, n))


def _row_sources(x, n_tiles, block_cols, col_of, tm=ROW_TILE):
    if not isinstance(x, tuple):
        return [x], [pl.BlockSpec((tm, block_cols), lambda *g: (g[-1], col_of(*g)))], None
    n_first = x[0].shape[0] // tm
    assert x[0].shape[0] % tm == 0 and x[1].shape[0] == (n_tiles - n_first) * tm
    specs = [pl.BlockSpec((tm, block_cols), lambda *g: (jnp.minimum(g[-1], n_first - 1), col_of(*g))),
             pl.BlockSpec((tm, block_cols), lambda *g: (jnp.maximum(g[-1] - n_first, 0), col_of(*g)))]
    return list(x), specs, n_first


def _pick_rows(x_refs, n_first, i):
    if n_first is None:
        return x_refs[0][...]
    return jnp.where(i < n_first, x_refs[0][...], x_refs[1][...])


def _normmod_kernel(*refs, n_first):
    *x_refs, g_ref, shift_ref, scale_ref, o_ref = refs
    x = _pick_rows(x_refs, n_first, pl.program_id(0))
    y = x * lax.rsqrt(jnp.mean(x * x, axis=-1, keepdims=True) + EPS) * g_ref[...]
    o_ref[...] = (y * (1.0 + scale_ref[0]) + shift_ref[0]).astype(o_ref.dtype)


def norm_mod(x, g, mods, shift_idx, scale_idx, seg_of_tile, n_tiles):
    d = g.shape[0]
    xs, x_specs, n_first = _row_sources(x, n_tiles, d, lambda i: 0)
    return pl.pallas_call(
        functools.partial(_normmod_kernel, n_first=n_first),
        grid=(n_tiles,),
        in_specs=x_specs + [
            pl.BlockSpec((1, d), lambda i: (0, 0)),
            pl.BlockSpec((1, 1, d), lambda i: (seg_of_tile(i), 0, shift_idx)),
            pl.BlockSpec((1, 1, d), lambda i: (seg_of_tile(i), 0, scale_idx)),
        ],
        out_specs=pl.BlockSpec((ROW_TILE, d), lambda i: (i, 0)),
        out_shape=jax.ShapeDtypeStruct((n_tiles * ROW_TILE, d), BF16),
        compiler_params=_params("arbitrary"),
        name="norm_mod",
    )(*xs, g.reshape(1, d), mods, mods)


class _WStream(typing.NamedTuple):
    hbm: typing.Any
    layer: int
    col0: int
    wbuf: typing.Any
    stage: typing.Any
    sem: typing.Any


def _n_chunks(k, n_row_tiles):
    for nk in range(min(MAX_W_CHUNKS, W_STAGE_SLOTS * (n_row_tiles - 1)), 0, -1):
        if k % nk == 0 and (k // nk) % BF16_SUBLANES == 0:
            return nk
    raise ValueError(f"no weight chunking for K={k} with {n_row_tiles} row tiles")


def _big_row_tile(n_rows):
    for tm in range(MAX_ROW_TILE, 0, -QK_SUB_BLOCKS * BF16_SUBLANES):
        if n_rows % tm == 0:
            return tm
    raise ValueError(f"no row tile for {n_rows} rows")


def _w_copy(st, tile, chunk, slot):
    _, ck, tn = st.stage.shape
    rows = pl.ds(pl.multiple_of(chunk * ck, BF16_SUBLANES), ck)
    cols = pl.ds(pl.multiple_of(st.col0 + tile * tn, LANES), tn)
    return pltpu.make_async_copy(st.hbm.at[st.layer, rows, cols], st.stage.at[slot], st.sem.at[slot])


def _w_round(st, buf, chunk, slot):
    ck = st.stage.shape[1]
    rows = pl.ds(pl.multiple_of(chunk * ck, BF16_SUBLANES), ck)
    st.wbuf[buf, rows, :] = st.stage[slot].astype(st.wbuf.dtype)


def _weight_pipeline(streams, nj, ni):
    j, i = pl.program_id(0), pl.program_id(1)
    nk = streams[0].wbuf.shape[1] // streams[0].stage.shape[1]
    cps = max(1, -(-nk // max(ni - 1, 1)))
    assert cps <= W_STAGE_SLOTS and (nj == 1 or (ni - 1) * cps >= nk)

    @pl.when((j == 0) & (i == 0))
    def _prime():
        for st in streams:
            _w_copy(st, 0, 0, 0).start()
        for c in range(nk):
            for st in streams:
                if c + 1 < nk:
                    _w_copy(st, 0, c + 1, (c + 1) % W_STAGE_SLOTS).start()
                _w_copy(st, 0, c, c % W_STAGE_SLOTS).wait()
                _w_round(st, 0, c, c % W_STAGE_SLOTS)

    @pl.when(j + 1 < nj)
    def _prefetch():
        for s in range(cps):
            prev = (i - 1) * cps + s

            @pl.when((i >= 1) & (prev < nk))
            def _(prev=prev, s=s):
                for st in streams:
                    _w_copy(st, j + 1, prev, s).wait()
                    _w_round(st, (j + 1) % 2, prev, s)

        for s in range(cps):
            cur = i * cps + s

            @pl.when(cur < nk)
            def _(cur=cur, s=s):
                for st in streams:
                    _w_copy(st, j + 1, cur, s).start()

    return j % 2


def _w_scratch(k, tn, nk):
    return [pltpu.VMEM((2, k, tn), BF16), pltpu.VMEM((W_STAGE_SLOTS, k // nk, tn), F32),
            pltpu.SemaphoreType.DMA((W_STAGE_SLOTS,))]


_HBM = pl.BlockSpec(memory_space=pl.ANY)


def _row_halves(ref, n=2):
    sub = ref.shape[0] // n
    assert sub * n == ref.shape[0] and sub % BF16_SUBLANES == 0
    return tuple(slice(r * sub, (r + 1) * sub) for r in range(n))


def _swiglu_kernel(a_ref, w_hbm, o_ref, wg, sg, semg, wu, su, semu, *, layer, nj, ni):
    tn = wg.shape[2]
    buf = _weight_pipeline([_WStream(w_hbm, layer, 0, wg, sg, semg),
                            _WStream(w_hbm, layer, nj * tn, wu, su, semu)], nj, ni)
    for rows in _row_halves(a_ref):
        a = a_ref[rows, :]
        g = _dot(a, wg[buf])
        u = _dot(a, wu[buf])
        o_ref[rows, :] = (g * _sigmoid(g) * u).astype(o_ref.dtype)


def ffn_up(h, w_in, layer, n_rows, tn=512):
    k = h.shape[1]
    f = w_in.shape[2] // 2
    nj = f // tn
    tm = _big_row_tile(n_rows)
    ni = n_rows // tm
    nk = _n_chunks(k, ni)
    return pl.pallas_call(
        functools.partial(_swiglu_kernel, layer=layer, nj=nj, ni=ni),
        grid=(nj, ni),
        in_specs=[pl.BlockSpec((tm, k), lambda j, i: (i, 0)), _HBM],
        out_specs=pl.BlockSpec((tm, tn), lambda j, i: (i, j)),
        out_shape=jax.ShapeDtypeStruct((n_rows, f), BF16),
        scratch_shapes=_w_scratch(k, tn, nk) + _w_scratch(k, tn, nk),
        compiler_params=_params("arbitrary", "arbitrary"),
        name="ffn_up",
    )(h, w_in)


def _resid_kernel(a_ref, w_hbm, *refs, layer, nj, ni, coef, n_first):
    *x_refs, gate_ref, o_ref, wb, sb, semb = refs
    buf = _weight_pipeline([_WStream(w_hbm, layer, 0, wb, sb, semb)], nj, ni)
    y = _dot(a_ref[...], wb[buf])
    o_ref[...] = _pick_rows(x_refs, n_first, pl.program_id(1)) + (coef * gate_ref[0]) * y


def resid_matmul(a, w, layer, x, mods, gate_idx, coef, seg_of_tile, n_tiles, tn, tm=ROW_TILE):
    _, k, n = w.shape
    tn = min(tn, n)
    nj = n // tn
    assert ROW_TILE % tm == 0
    ni = n_tiles * (ROW_TILE // tm)
    nk = _n_chunks(k, ni)
    xs, x_specs, n_first = _row_sources(x, ni, tn, lambda j, i: j, tm)
    return pl.pallas_call(
        functools.partial(_resid_kernel, layer=layer, nj=nj, ni=ni, coef=coef, n_first=n_first),
        grid=(nj, ni),
        in_specs=[pl.BlockSpec((tm, k), lambda j, i: (i, 0)), _HBM] + x_specs + [
            pl.BlockSpec((1, 1, tn), lambda j, i: (seg_of_tile(i * tm // ROW_TILE), 0, gate_idx * nj + j)),
        ],
        out_specs=pl.BlockSpec((tm, tn), lambda j, i: (i, j)),
        out_shape=jax.ShapeDtypeStruct((n_tiles * ROW_TILE, n), F32),
        scratch_shapes=_w_scratch(k, tn, nk),
        compiler_params=_params("arbitrary", "arbitrary"),
        name="resid_matmul",
    )(a, w, *xs, mods)


def _proj_kernel(a_ref, w_hbm, o_ref, wb, sb, semb, *, layer, col0, nj, ni):
    buf = _weight_pipeline([_WStream(w_hbm, layer, col0, wb, sb, semb)], nj, ni)
    for rows in _row_halves(a_ref):
        o_ref[rows, :] = _dot(a_ref[rows, :], wb[buf]).astype(o_ref.dtype)


def proj_plain(h, w, layer, col0, width, n_rows, tn=512):
    k = h.shape[1]
    nj = width // tn
    tm = _big_row_tile(n_rows)
    ni = n_rows // tm
    nk = _n_chunks(k, ni)
    return pl.pallas_call(
        functools.partial(_proj_kernel, layer=layer, col0=col0, nj=nj, ni=ni),
        grid=(nj, ni),
        in_specs=[pl.BlockSpec((tm, k), lambda j, i: (i, 0)), _HBM],
        out_specs=pl.BlockSpec((tm, tn), lambda j, i: (i, j)),
        out_shape=jax.ShapeDtypeStruct((n_rows, width), BF16),
        scratch_shapes=_w_scratch(k, tn, nk),
        compiler_params=_params("arbitrary", "arbitrary"),
        name="proj_plain",
    )(h, w)


def _swap_pairs(y):
    lane = lax.broadcasted_iota(jnp.int32, y.shape, 1)
    nxt = pltpu.roll(y, LANES - 1, 1)
    prv = pltpu.roll(y, 1, 1)
    return jnp.where(lane % 2 == 0, nxt, prv)


def _proj_qk_kernel(a_ref, w_hbm, gain_ref, *rest, layer, col0, nj, ni, head_dim, rope):
    if rope:
        cos_ref, sin_ref, o_ref, wb, sb, semb = rest
    else:
        o_ref, wb, sb, semb = rest
    buf = _weight_pipeline([_WStream(w_hbm, layer, col0, wb, sb, semb)], nj, ni)
    tn = wb.shape[2]
    for rows in _row_halves(a_ref, QK_SUB_BLOCKS):
        acc = _dot(a_ref[rows, :], wb[buf])
        for s in range(tn // LANES):
            sl = slice(s * LANES, (s + 1) * LANES)
            y = acc[:, sl]
            sq = y * y
            if head_dim == LANES:
                ms = jnp.mean(sq, axis=-1, keepdims=True)
            else:
                lane = lax.broadcasted_iota(jnp.int32, y.shape, 1)
                low = lane < head_dim
                s_low = jnp.sum(jnp.where(low, sq, 0.0), axis=-1, keepdims=True)
                s_high = jnp.sum(jnp.where(low, 0.0, sq), axis=-1, keepdims=True)
                ms = jnp.where(low, s_low, s_high) * (1.0 / head_dim)
            y = y * lax.rsqrt(ms + EPS) * gain_ref[:, sl]
            if rope:
                y = y * cos_ref[rows, :] + _swap_pairs(y) * sin_ref[rows, :]
            o_ref[rows, sl] = y.astype(o_ref.dtype)


def proj_qk(h, w, layer, col0, width, gain_row, head_dim, rope_tabs, n_rows, tn=512):
    k = h.shape[1]
    nj = width // tn
    tm = _big_row_tile(n_rows)
    ni = n_rows // tm
    nk = _n_chunks(k, ni)
    rope = rope_tabs is not None
    in_specs = [
        pl.BlockSpec((tm, k), lambda j, i: (i, 0)),
        _HBM,
        pl.BlockSpec((1, tn), lambda j, i: (0, j)),
    ]
    args = [h, w, gain_row]
    if rope:
        in_specs += [pl.BlockSpec((tm, LANES), lambda j, i: (i, 0))] * 2
        args += list(rope_tabs)
    return pl.pallas_call(
        functools.partial(_proj_qk_kernel, layer=layer, col0=col0, nj=nj, ni=ni, head_dim=head_dim, rope=rope),
        grid=(nj, ni),
        in_specs=in_specs,
        out_specs=pl.BlockSpec((tm, tn), lambda j, i: (i, j)),
        out_shape=jax.ShapeDtypeStruct((n_rows, width), BF16),
        scratch_shapes=_w_scratch(k, tn, nk),
        compiler_params=_params("arbitrary", "arbitrary"),
        name="proj_qk",
    )(*args)


def _merge_kernel(oa_ref, ob_ref, oc_ref, wa_hbm, wb_hbm, wc_hbm, ga_ref, gb_ref, gc_ref, o_ref,
                  wa, sa, sema, wb, sb, semb, wc, sc, semc, *, layer, nj, ni):
    buf = _weight_pipeline([_WStream(wa_hbm, layer, 0, wa, sa, sema), _WStream(wb_hbm, layer, 0, wb, sb, semb),
                            _WStream(wc_hbm, layer, 0, wc, sc, semc)], nj, ni)
    for rows in _row_halves(o_ref):
        m = _sigmoid(ga_ref[rows, :].astype(F32)) * _dot(oa_ref[rows, :], wa[buf])
        m += _sigmoid(gb_ref[rows, :].astype(F32)) * _dot(ob_ref[rows, :], wb[buf])
        m += _sigmoid(gc_ref[rows, :].astype(F32)) * _dot(oc_ref[rows, :], wc[buf])
        o_ref[rows, :] = m.astype(o_ref.dtype)


def gated_merge(oa, ob, oc, wa, wb, wc, layer, gates, n_tiles, tn=1024):
    d = wa.shape[2]
    tn = min(tn, d)
    nj = d // tn
    nk = min(_n_chunks(w.shape[1], n_tiles) for w in (wa, wb, wc))
    assert all(w.shape[1] % nk == 0 and (w.shape[1] // nk) % BF16_SUBLANES == 0 for w in (wa, wb, wc))
    act = lambda o: pl.BlockSpec((ROW_TILE, o.shape[1]), lambda j, i: (i, 0))
    gate = lambda b: pl.BlockSpec((ROW_TILE, tn), lambda j, i: (i, j + b * nj))
    return pl.pallas_call(
        functools.partial(_merge_kernel, layer=layer, nj=nj, ni=n_tiles),
        grid=(nj, n_tiles),
        in_specs=[act(oa), act(ob), act(oc), _HBM, _HBM, _HBM, gate(0), gate(1), gate(2)],
        out_specs=pl.BlockSpec((ROW_TILE, tn), lambda j, i: (i, j)),
        out_shape=jax.ShapeDtypeStruct((n_tiles * ROW_TILE, d), BF16),
        scratch_shapes=sum((_w_scratch(w.shape[1], tn, nk) for w in (wa, wb, wc)), []),
        compiler_params=_params("arbitrary", "arbitrary"),
        name="gated_merge",
    )(oa, ob, oc, wa, wb, wc, gates, gates, gates)


def _head_slices(n):
    return [slice(j * HEAD_DIM, (j + 1) * HEAD_DIM) for j in range(n)]


def _build_vaug(vaug_ref, vl_ref, vc_ref):
    seq = vl_ref.shape[0]
    for j, hs in enumerate(_head_slices(vaug_ref.shape[0])):
        vaug_ref[j, 0:seq, 0:HEAD_DIM] = vl_ref[:, hs]
        vaug_ref[j, seq:, 0:HEAD_DIM] = vc_ref[:, hs]
        vaug_ref[j, :, HEAD_DIM:] = jnp.ones((vaug_ref.shape[1], HEAD_DIM), vaug_ref.dtype)


def _attend(q, parts):
    acc = m_run = None
    for k, vaug, bias in parts:
        s = _dot_t(q, k)
        if bias is not None:
            s = s + bias
        m_new = jnp.max(s, axis=-1, keepdims=True)
        if acc is not None:
            m_new = jnp.maximum(m_run, m_new)
        o = _dot(jnp.exp2(s - m_new).astype(vaug.dtype), vaug)
        acc = o if acc is None else jnp.exp2(m_run - m_new) * acc + o
        m_run = m_new
    return acc[:, :HEAD_DIM] / acc[:, HEAD_DIM:]


def _key_parts(kl_ref, kc_ref, vaug_ref, latent, j, hs):
    seq = kl_ref.shape[0]
    parts = []
    if latent:
        for c in range(seq // KEY_CHUNK):
            sl = slice(c * KEY_CHUNK, (c + 1) * KEY_CHUNK)
            parts.append((kl_ref[sl, hs], vaug_ref[j, sl, :], None))
    parts.append((kc_ref[:, hs], vaug_ref[j, seq:, :], None))
    return parts


def _gqa_kernel(q_ref, kl_ref, kc_ref, vl_ref, vc_ref, o_ref, vaug_ref, *, group, n_lat, with_ctx):
    qt = pl.program_id(2)
    pl.when(qt == 0)(lambda: _build_vaug(vaug_ref, vl_ref, vc_ref))

    def run(latent):
        for j, hs in enumerate(_head_slices(vaug_ref.shape[0])):
            cols = [slice((j * group + g) * HEAD_DIM, (j * group + g + 1) * HEAD_DIM) for g in range(group)]
            q = jnp.concatenate([q_ref[:, c] for c in cols], axis=0)
            o = _attend(q, _key_parts(kl_ref, kc_ref, vaug_ref, latent, j, hs))
            for g, c in enumerate(cols):
                o_ref[:, c] = o[g * Q_TILE:(g + 1) * Q_TILE].astype(o_ref.dtype)

    if with_ctx:
        pl.when(qt < n_lat)(lambda: run(True))
        pl.when(qt >= n_lat)(lambda: run(False))
    else:
        run(True)


def _q_row_block(b, qt, n_lat, n_batch):
    return jnp.where(qt < n_lat, b * n_lat + qt, n_batch * n_lat + b)


def gqa_attention(q, k, v, n_batch, seq, ctx_len, group, with_ctx):
    rows = q.shape[0] if with_ctx else n_batch * seq
    kvh = k.shape[1] // HEAD_DIM
    n_lat = seq // Q_TILE
    ctx_blk0 = n_batch * seq // ctx_len
    nh = GQA_KV_HEADS if kvh % GQA_KV_HEADS == 0 else 1
    gw, kw = nh * group * HEAD_DIM, nh * HEAD_DIM
    qmap = lambda b, h, t: (_q_row_block(b, t, n_lat, n_batch), h)
    lat = pl.BlockSpec((seq, kw), lambda b, h, t: (b, h))
    ctx = pl.BlockSpec((ctx_len, kw), lambda b, h, t: (ctx_blk0 + b, h))
    return pl.pallas_call(
        functools.partial(_gqa_kernel, group=group, n_lat=n_lat, with_ctx=with_ctx),
        grid=(n_batch, kvh // nh, n_lat + int(with_ctx)),
        in_specs=[pl.BlockSpec((Q_TILE, gw), qmap), lat, ctx, lat, ctx],
        out_specs=pl.BlockSpec((Q_TILE, gw), qmap),
        out_shape=jax.ShapeDtypeStruct((rows, q.shape[1]), BF16),
        scratch_shapes=[pltpu.VMEM((nh, seq + ctx_len, 2 * HEAD_DIM), BF16)],
        compiler_params=_params("arbitrary", "arbitrary", "arbitrary"),
        name="gqa_attention",
    )(q, k, k, v, v)


def _diff_kernel(q_ref, kl_ref, kc_ref, vl_ref, vc_ref, lam_ref, g_ref, o_ref, vaug_ref, *,
                 lam_init, n_lat, with_ctx):
    qt = pl.program_id(2)
    pl.when(qt == 0)(lambda: _build_vaug(vaug_ref, vl_ref, vc_ref))
    lp = lam_ref[...]
    lam = (jnp.exp(jnp.sum(lp[0:1] * lp[1:2], axis=-1, keepdims=True))
           - jnp.exp(jnp.sum(lp[2:3] * lp[3:4], axis=-1, keepdims=True)) + lam_init)

    def run(latent):
        for j, hs in enumerate(_head_slices(vaug_ref.shape[0])):
            q = q_ref[:, hs]
            lane = lax.broadcasted_iota(jnp.int32, q.shape, 1)
            zero = jnp.zeros_like(q)
            q2 = jnp.concatenate([jnp.where(lane < C_QK_DIM, q, zero), jnp.where(lane < C_QK_DIM, zero, q)],
                                 axis=0)
            o2 = _attend(q2, _key_parts(kl_ref, kc_ref, vaug_ref, latent, j, hs))
            o = o2[:Q_TILE] - lam * o2[Q_TILE:]
            o = o * lax.rsqrt(jnp.mean(o * o, axis=-1, keepdims=True) + EPS) * g_ref[...]
            o_ref[:, hs] = (o * (1.0 - lam_init)).astype(o_ref.dtype)

    if with_ctx:
        pl.when(qt < n_lat)(lambda: run(True))
        pl.when(qt >= n_lat)(lambda: run(False))
    else:
        run(True)


def diff_attention(q, k, v, lam_params, subln, lam_init, n_batch, seq, ctx_len, with_ctx):
    rows = q.shape[0] if with_ctx else n_batch * seq
    n_heads = v.shape[1] // HEAD_DIM
    n_lat = seq // Q_TILE
    ctx_blk0 = n_batch * seq // ctx_len
    nh = DIFF_HEADS if n_heads % DIFF_HEADS == 0 else 1
    hw = nh * HEAD_DIM
    qmap = lambda b, h, t: (_q_row_block(b, t, n_lat, n_batch), h)
    lat = pl.BlockSpec((seq, hw), lambda b, h, t: (b, h))
    ctx = pl.BlockSpec((ctx_len, hw), lambda b, h, t: (ctx_blk0 + b, h))
    full = lambda a: pl.BlockSpec(a.shape, lambda b, h, t: (0,) * a.ndim)
    subln = subln.reshape(1, -1)
    return pl.pallas_call(
        functools.partial(_diff_kernel, lam_init=lam_init, n_lat=n_lat, with_ctx=with_ctx),
        grid=(n_batch, n_heads // nh, n_lat + int(with_ctx)),
        in_specs=[pl.BlockSpec((Q_TILE, hw), qmap), lat, ctx, lat, ctx, full(lam_params), full(subln)],
        out_specs=pl.BlockSpec((Q_TILE, hw), qmap),
        out_shape=jax.ShapeDtypeStruct((rows, v.shape[1]), BF16),
        scratch_shapes=[pltpu.VMEM((nh, seq + ctx_len, 2 * HEAD_DIM), BF16)],
        compiler_params=_params("arbitrary", "arbitrary", "arbitrary"),
        name="diff_attention",
    )(q, k, k, v, v, lam_params, subln)


def _na_plan(rows):
    kh = min(NA_KH, rows)
    band = min(NA_ROWS + kh, rows)
    patterns, types = [], []
    for blk in range(rows // NA_ROWS):
        r0 = blk * NA_ROWS
        bs = min(max(r0 - kh // 2, 0), rows - band)
        pat = []
        for qr in range(r0, r0 + NA_ROWS):
            rs = min(max(qr - kh // 2, 0), rows - kh)
            pat.append(tuple((kr - qr + kh - 1) if rs <= kr < rs + kh else -1 for kr in range(bs, bs + band)))
        pat = tuple(pat)
        if pat not in patterns:
            patterns.append(pat)
        types.append(patterns.index(pat))
    return band, np.asarray(types, np.int32), patterns


def _na_bias_kernel(rb_ref, o_ref, *, kh, patterns):
    h = pl.program_id(0)
    n_dc = 2 * NA_KW - 1
    base = h * ((2 * NA_KH - 1) * n_dc)
    qc = lax.broadcasted_iota(jnp.int32, (GRID_W, GRID_W), 0)
    kc = lax.broadcasted_iota(jnp.int32, (GRID_W, GRID_W), 1)
    dc = kc - qc
    cs = jnp.clip(qc - NA_KW // 2, 0, GRID_W - NA_KW)
    col_ok = (kc >= cs) & (kc < cs + NA_KW)
    neg = jnp.full((GRID_W, GRID_W), NEG_BIAS, F32)
    used = sorted({a for pat in patterns for row in pat for a in row if a >= 0})
    toep = {}
    for a in used:
        a_full = a + (NA_KH - kh)
        t = neg
        for b in range(n_dc):
            t = jnp.where(dc == b - (NA_KW - 1), rb_ref[base + a_full * n_dc + b] * LOG2E, t)
        toep[a] = jnp.where(col_ok, t, neg)
    for t_id, pat in enumerate(patterns):
        for qr, row in enumerate(pat):
            blocks = [toep[a] if a >= 0 else neg for a in row]
            o_ref[0, t_id, qr * GRID_W:(qr + 1) * GRID_W, :] = jnp.concatenate(blocks, axis=1)


def na_bias_tables(rel_bias, rows):
    n_heads = rel_bias.shape[0]
    kh = min(NA_KH, rows)
    band, _, patterns = _na_plan(rows)
    shape = (n_heads, len(patterns), Q_TILE, band * GRID_W)
    return pl.pallas_call(
        functools.partial(_na_bias_kernel, kh=kh, patterns=patterns),
        grid=(n_heads,),
        in_specs=[pl.BlockSpec(memory_space=pltpu.SMEM)],
        out_specs=pl.BlockSpec((1,) + shape[1:], lambda h: (h, 0, 0, 0)),
        out_shape=jax.ShapeDtypeStruct(shape, F32),
        compiler_params=_params("arbitrary"),
        name="na_bias_tables",
    )(rel_bias.reshape(-1))


def _na_kernel(types_ref, q_ref, kl_ref, kc_ref, vl_ref, vc_ref, bias_ref, o_ref, vaug_ref, *,
               rows, band, n_lat, with_ctx):
    del types_ref
    rt = pl.program_id(2)
    kh = min(NA_KH, rows)
    seq = kl_ref.shape[0]
    heads = _head_slices(vaug_ref.shape[0])
    pl.when(rt == 0)(lambda: _build_vaug(vaug_ref, vl_ref, vc_ref))
    ctx_part = lambda j, hs: (kc_ref[:, hs], vaug_ref[j, seq:, :], None)

    def latent():
        bs = jnp.clip(rt * NA_ROWS - kh // 2, 0, rows - band)
        band_sl = pl.ds(pl.multiple_of(bs * GRID_W, GRID_W), band * GRID_W)
        for j, hs in enumerate(heads):
            parts = [(kl_ref[band_sl, hs], vaug_ref[j, band_sl, :], bias_ref[j, 0]), ctx_part(j, hs)]
            o_ref[:, hs] = _attend(q_ref[:, hs], parts).astype(o_ref.dtype)

    def context():
        for j, hs in enumerate(heads):
            o_ref[:, hs] = _attend(q_ref[:, hs], [ctx_part(j, hs)]).astype(o_ref.dtype)

    if with_ctx:
        pl.when(rt < n_lat)(latent)
        pl.when(rt >= n_lat)(context)
    else:
        latent()


def neighbourhood_attention(q, k, v, bias_tabs, n_batch, seq, ctx_len, with_ctx):
    rows_tok = q.shape[0] if with_ctx else n_batch * seq
    n_heads = q.shape[1] // HEAD_DIM
    rows = seq // GRID_W
    band, types, _ = _na_plan(rows)
    n_lat = seq // Q_TILE
    ctx_blk0 = n_batch * seq // ctx_len
    types = jnp.asarray(np.concatenate([types, types[-1:]]))
    nh = NA_HEADS if n_heads % NA_HEADS == 0 else 1
    hw = nh * HEAD_DIM
    qmap = lambda b, h, t, ty: (_q_row_block(b, t, n_lat, n_batch), h)
    lat = pl.BlockSpec((seq, hw), lambda b, h, t, ty: (b, h))
    ctx = pl.BlockSpec((ctx_len, hw), lambda b, h, t, ty: (ctx_blk0 + b, h))
    bias = pl.BlockSpec((nh, 1, Q_TILE, band * GRID_W), lambda b, h, t, ty: (h, ty[t], 0, 0))
    return pl.pallas_call(
        functools.partial(_na_kernel, rows=rows, band=band, n_lat=n_lat, with_ctx=with_ctx),
        grid_spec=pltpu.PrefetchScalarGridSpec(
            num_scalar_prefetch=1,
            grid=(n_batch, n_heads // nh, n_lat + int(with_ctx)),
            in_specs=[pl.BlockSpec((Q_TILE, hw), qmap), lat, ctx, lat, ctx, bias],
            out_specs=pl.BlockSpec((Q_TILE, hw), qmap),
            scratch_shapes=[pltpu.VMEM((nh, seq + ctx_len, 2 * HEAD_DIM), BF16)],
        ),
        out_shape=jax.ShapeDtypeStruct((rows_tok, q.shape[1]), BF16),
        compiler_params=_params("arbitrary", "arbitrary", "arbitrary"),
        name="neighbourhood_attention",
    )(types, q, k, k, v, v, bias_tabs)


def _rope_tables(seq, dim, n_batch, n_ctx_rows):
    t = jnp.arange(seq, dtype=jnp.int32)
    row = (t // GRID_W).astype(F32)
    col = (t % GRID_W).astype(F32)
    n_pairs = dim // 4
    inv = ROPE_THETA ** (-jnp.arange(n_pairs, dtype=F32) / n_pairs)
    ang = jnp.concatenate([row[:, None] * inv, col[:, None] * inv], axis=-1)
    cos = jnp.repeat(jnp.cos(ang), 2, axis=-1)
    sin = jnp.stack([-jnp.sin(ang), jnp.sin(ang)], axis=-1).reshape(seq, dim)
    reps = LANES // dim
    cos, sin = jnp.tile(cos, (n_batch, reps)), jnp.tile(sin, (n_batch, reps))
    cos = jnp.concatenate([cos, jnp.ones((n_ctx_rows, LANES), F32)], axis=0)
    sin = jnp.concatenate([sin, jnp.zeros((n_ctx_rows, LANES), F32)], axis=0)
    return cos, sin


def kernel(x, c, ctx, c_ctx, w_ada, b_ada, norm_g, ffn1_in, ffn1_out, ffn2_in, ffn2_out, w_in,
           qk_gain_a, qk_gain_b, qk_gain_c, na_rel_bias, diff_lambda, diff_subln,
           w_br_a, w_br_b, w_br_c, w_out):
    n_batch, seq, d = x.shape
    ctx_len = ctx.shape[1]
    depth = w_ada.shape[0]
    a_q_w, b_w, c_w = w_br_a.shape[1], w_br_b.shape[1], w_br_c.shape[1]
    a_kv_w = (w_in.shape[2] - a_q_w - 3 * b_w - 3 * c_w - 3 * d) // 2
    group = a_q_w // a_kv_w
    assert ctx_len == Q_TILE and seq % ROW_TILE == 0 and (n_batch * ctx_len) % ROW_TILE == 0
    assert seq % KEY_CHUNK == 0 and n_batch + 1 <= MOD_ROWS
    n_lat_rows = n_batch * seq
    n_ctx_rows = n_batch * ctx_len
    lat_tiles = n_lat_rows // ROW_TILE
    all_tiles = lat_tiles + n_ctx_rows // ROW_TILE
    n_all_rows = n_lat_rows + n_ctx_rows
    tiles_per_batch = seq // ROW_TILE
    seg_of_tile = lambda i: jnp.minimum(i // tiles_per_batch, n_batch)

    sizes = (a_q_w, a_kv_w, a_kv_w, b_w, b_w, b_w, c_w, c_w, c_w, 3 * d)
    offs = [int(o) for o in np.cumsum((0,) + sizes)]

    cvec = jnp.concatenate([c, c_ctx[None], jnp.zeros((MOD_ROWS - n_batch - 1, d), F32)], axis=0)
    mods_all = ada_modulation(cvec, w_ada, b_ada)

    rope_a = _rope_tables(seq, HEAD_DIM, n_batch, n_ctx_rows)
    rope_c = _rope_tables(seq, C_QK_DIM, n_batch, n_ctx_rows)
    rows = seq // GRID_W

    xs = (x.reshape(n_lat_rows, d), ctx.reshape(n_ctx_rows, d))

    w1i, w1o, w2i, w2o = ffn1_in, ffn1_out, ffn2_in, ffn2_out
    wp, wa, wb, wc, wo = w_in, w_br_a, w_br_b, w_br_c, w_out

    def gain_row(gain, width, scale):
        return jnp.tile(gain * scale, width // gain.shape[0]).reshape(1, width)

    sa, sc = HEAD_DIM ** -0.5 * LOG2E, C_QK_DIM ** -0.5 * LOG2E

    for l in range(depth):
        last = l == depth - 1
        with_ctx = not last
        lam_init = 0.8 - 0.6 * math.exp(-0.3 * l)
        mods = mods_all[l].reshape(MOD_ROWS, 1, N_MOD * d)

        h = norm_mod(xs, norm_g[l, 0], mods, 0, 1, seg_of_tile, all_tiles)
        a = ffn_up(h, w1i, l, n_all_rows)
        xs = resid_matmul(a, w1o, l, xs, mods, 2, 0.5, seg_of_tile, all_tiles, tn=1024, tm=FFN_DOWN_ROWS)

        h = norm_mod(xs, norm_g[l, 1], mods, 3, 4, seg_of_tile, all_tiles)
        qa = proj_qk(h, wp, l, offs[0], a_q_w, gain_row(qk_gain_a[l, 0], a_q_w, sa), HEAD_DIM, rope_a, n_all_rows)
        ka = proj_qk(h, wp, l, offs[1], a_kv_w, gain_row(qk_gain_a[l, 1], a_kv_w, 1.0), HEAD_DIM, rope_a, n_all_rows)
        va = proj_plain(h, wp, l, offs[2], a_kv_w, n_all_rows)
        qb = proj_qk(h, wp, l, offs[3], b_w, gain_row(qk_gain_b[l, 0], b_w, sa), HEAD_DIM, None, n_all_rows)
        kb = proj_qk(h, wp, l, offs[4], b_w, gain_row(qk_gain_b[l, 1], b_w, 1.0), HEAD_DIM, None, n_all_rows)
        vb = proj_plain(h, wp, l, offs[5], b_w, n_all_rows)
        qc = proj_qk(h, wp, l, offs[6], c_w, gain_row(qk_gain_c[l, 0], c_w, sc), C_QK_DIM, rope_c, n_all_rows)
        kc = proj_qk(h, wp, l, offs[7], c_w, gain_row(qk_gain_c[l, 1], c_w, 1.0), C_QK_DIM, rope_c, n_all_rows)
        vc = proj_plain(h, wp, l, offs[8], c_w, n_all_rows)
        mix_tiles = all_tiles if with_ctx else lat_tiles
        mix_rows = mix_tiles * ROW_TILE
        gates = proj_plain(h, wp, l, offs[9], 3 * d, mix_rows, tn=min(1024, d))

        o_a = gqa_attention(qa, ka, va, n_batch, seq, ctx_len, group, with_ctx)
        bias_tabs = na_bias_tables(na_rel_bias[l], rows)
        o_b = neighbourhood_attention(qb, kb, vb, bias_tabs, n_batch, seq, ctx_len, with_ctx)
        o_c = diff_attention(qc, kc, vc, diff_lambda[l], diff_subln[l], lam_init, n_batch, seq, ctx_len, with_ctx)

        m = gated_merge(o_a, o_b, o_c, wa, wb, wc, l, gates, mix_tiles)
        xs = resid_matmul(m, wo, l, xs, mods, 5, 1.0, seg_of_tile, mix_tiles, tn=1024)

        h = norm_mod(xs, norm_g[l, 2], mods, 6, 7, seg_of_tile, mix_tiles)
        a = ffn_up(h, w2i, l, mix_rows)
        xs = resid_matmul(a, w2o, l, xs, mods, 8, 0.5, seg_of_tile, mix_tiles, tn=1024, tm=FFN_DOWN_ROWS)

    return xs.reshape(n_batch, seq, d)
```

```python
import functools
import math
import typing

import numpy as np
import jax
import jax.numpy as jnp
from jax import lax
from jax.experimental import pallas as pl
from jax.experimental.pallas import tpu as pltpu

F32 = jnp.float32
BF16 = jnp.bfloat16

GRID_W = 64
HEAD_DIM = 128
C_QK_DIM = 64
NA_KH = 8
NA_KW = 16
ROPE_THETA = 10000.0
EPS = 1e-6
N_MOD = 9
LANES = 128
MOD_ROWS = 8
NEG_BIAS = -1e30
LOG2E = math.log2(math.e)
KEY_CHUNK = 512

V7X_VMEM_BYTES = 64 * 1024 * 1024
VMEM_LIMIT = (V7X_VMEM_BYTES * 7) // 8

ROW_TILE = 512
BF16_SUBLANES = 16
MAX_W_CHUNKS = 8
W_STAGE_SLOTS = 2
FFN_DOWN_ROWS = 256
QK_SUB_BLOCKS = 4
MAX_ROW_TILE = 1088
NA_HEADS = 4
DIFF_HEADS = 4
GQA_KV_HEADS = 4
Q_TILE = 256
NA_ROWS = Q_TILE // GRID_W


def _params(*sem):
    return pltpu.CompilerParams(dimension_semantics=sem, vmem_limit_bytes=VMEM_LIMIT)


def _sigmoid(v):
    return 1.0 / (1.0 + jnp.exp(-v))


def _dot(a, b):
    return jnp.dot(a, b, preferred_element_type=F32)


def _dot_t(a, b):
    return lax.dot_general(a, b, (((1,), (1,)), ((), ())), preferred_element_type=F32)


def _ada_kernel(c_ref, w_ref, b_ref, o_ref):
    c = c_ref[...]
    a = (c * _sigmoid(c)).astype(BF16)
    o_ref[0] = _dot(a, w_ref[0].astype(BF16)) + b_ref[0]


def ada_modulation(cvec, w_ada, b_ada, tn=512):
    depth, d, n = w_ada.shape
    return pl.pallas_call(
        _ada_kernel,
        grid=(depth, n // tn),
        in_specs=[
            pl.BlockSpec((MOD_ROWS, d), lambda l, j: (0, 0)),
            pl.BlockSpec((1, d, tn), lambda l, j: (l, 0, j)),
            pl.BlockSpec((1, 1, tn), lambda l, j: (l, 0, j)),
        ],
        out_specs=pl.BlockSpec((1, MOD_ROWS, tn), lambda l, j: (l, 0, j)),
        out_shape=jax.ShapeDtypeStruct((depth, MOD_ROWS, n), F32),
        compiler_params=_params("arbitrary", "arbitrary"),
        name="ada_mod",
    )(cvec, w_ada, b_ada.reshape(depth, 1, n))


def _row_sources(x, n_tiles, block_cols, col_of, tm=ROW_TILE):
    if not isinstance(x, tuple):
        return [x], [pl.BlockSpec((tm, block_cols), lambda *g: (g[-1], col_of(*g)))], None
    n_first = x[0].shape[0] // tm
    assert x[0].shape[0] % tm == 0 and x[1].shape[0] == (n_tiles - n_first) * tm
    specs = [pl.BlockSpec((tm, block_cols), lambda *g: (jnp.minimum(g[-1], n_first - 1), col_of(*g))),
             pl.BlockSpec((tm, block_cols), lambda *g: (jnp.maximum(g[-1] - n_first, 0), col_of(*g)))]
    return list(x), specs, n_first


def _pick_rows(x_refs, n_first, i):
    if n_first is None:
        return x_refs[0][...]
    return jnp.where(i < n_first, x_refs[0][...], x_refs[1][...])


def _normmod_kernel(*refs, n_first):
    *x_refs, g_ref, shift_ref, scale_ref, o_ref = refs
    x = _pick_rows(x_refs, n_first, pl.program_id(0))
    y = x * lax.rsqrt(jnp.mean(x * x, axis=-1, keepdims=True) + EPS) * g_ref[...]
    o_ref[...] = (y * (1.0 + scale_ref[0]) + shift_ref[0]).astype(o_ref.dtype)


def norm_mod(x, g, mods, shift_idx, scale_idx, seg_of_tile, n_tiles):
    d = g.shape[0]
    xs, x_specs, n_first = _row_sources(x, n_tiles, d, lambda i: 0)
    return pl.pallas_call(
        functools.partial(_normmod_kernel, n_first=n_first),
        grid=(n_tiles,),
        in_specs=x_specs + [
            pl.BlockSpec((1, d), lambda i: (0, 0)),
            pl.BlockSpec((1, 1, d), lambda i: (seg_of_tile(i), 0, shift_idx)),
            pl.BlockSpec((1, 1, d), lambda i: (seg_of_tile(i), 0, scale_idx)),
        ],
        out_specs=pl.BlockSpec((ROW_TILE, d), lambda i: (i, 0)),
        out_shape=jax.ShapeDtypeStruct((n_tiles * ROW_TILE, d), BF16),
        compiler_params=_params("arbitrary"),
        name="norm_mod",
    )(*xs, g.reshape(1, d), mods, mods)


class _WStream(typing.NamedTuple):
    hbm: typing.Any
    layer: int
    col0: int
    wbuf: typing.Any
    stage: typing.Any
    sem: typing.Any


def _n_chunks(k, n_row_tiles):
    for nk in range(min(MAX_W_CHUNKS, W_STAGE_SLOTS * (n_row_tiles - 1)), 0, -1):
        if k % nk == 0 and (k // nk) % BF16_SUBLANES == 0:
            return nk
    raise ValueError(f"no weight chunking for K={k} with {n_row_tiles} row tiles")


def _big_row_tile(n_rows):
    for tm in range(MAX_ROW_TILE, 0, -QK_SUB_BLOCKS * BF16_SUBLANES):
        if n_rows % tm == 0:
            return tm
    raise ValueError(f"no row tile for {n_rows} rows")


def _w_copy(st, tile, chunk, slot):
    _, ck, tn = st.stage.shape
    rows = pl.ds(pl.multiple_of(chunk * ck, BF16_SUBLANES), ck)
    cols = pl.ds(pl.multiple_of(st.col0 + tile * tn, LANES), tn)
    return pltpu.make_async_copy(st.hbm.at[st.layer, rows, cols], st.stage.at[slot], st.sem.at[slot])


def _w_round(st, buf, chunk, slot):
    ck = st.stage.shape[1]
    rows = pl.ds(pl.multiple_of(chunk * ck, BF16_SUBLANES), ck)
    st.wbuf[buf, rows, :] = st.stage[slot].astype(st.wbuf.dtype)


def _weight_pipeline(streams, nj, ni):
    j, i = pl.program_id(0), pl.program_id(1)
    nk = streams[0].wbuf.shape[1] // streams[0].stage.shape[1]
    cps = max(1, -(-nk // max(ni - 1, 1)))
    assert cps <= W_STAGE_SLOTS and (nj == 1 or (ni - 1) * cps >= nk)

    @pl.when((j == 0) & (i == 0))
    def _prime():
        for st in streams:
            _w_copy(st, 0, 0, 0).start()
        for c in range(nk):
            for st in streams:
                if c + 1 < nk:
                    _w_copy(st, 0, c + 1, (c + 1) % W_STAGE_SLOTS).start()
                _w_copy(st, 0, c, c % W_STAGE_SLOTS).wait()
                _w_round(st, 0, c, c % W_STAGE_SLOTS)

    @pl.when(j + 1 < nj)
    def _prefetch():
        for s in range(cps):
            prev = (i - 1) * cps + s

            @pl.when((i >= 1) & (prev < nk))
            def _(prev=prev, s=s):
                for st in streams:
                    _w_copy(st, j + 1, prev, s).wait()
                    _w_round(st, (j + 1) % 2, prev, s)

        for s in range(cps):
            cur = i * cps + s

            @pl.when(cur < nk)
            def _(cur=cur, s=s):
                for st in streams:
                    _w_copy(st, j + 1, cur, s).start()

    return j % 2


def _w_scratch(k, tn, nk):
    return [pltpu.VMEM((2, k, tn), BF16), pltpu.VMEM((W_STAGE_SLOTS, k // nk, tn), F32),
            pltpu.SemaphoreType.DMA((W_STAGE_SLOTS,))]


_HBM = pl.BlockSpec(memory_space=pl.ANY)


def _row_halves(ref, n=2):
    sub = ref.shape[0] // n
    assert sub * n == ref.shape[0] and sub % BF16_SUBLANES == 0
    return tuple(slice(r * sub, (r + 1) * sub) for r in range(n))


def _swiglu_kernel(a_ref, w_hbm, o_ref, wg, sg, semg, wu, su, semu, *, layer, nj, ni):
    tn = wg.shape[2]
    buf = _weight_pipeline([_WStream(w_hbm, layer, 0, wg, sg, semg),
                            _WStream(w_hbm, layer, nj * tn, wu, su, semu)], nj, ni)
    for rows in _row_halves(a_ref):
        a = a_ref[rows, :]
        g = _dot(a, wg[buf])
        u = _dot(a, wu[buf])
        o_ref[rows, :] = (g * _sigmoid(g) * u).astype(o_ref.dtype)


def ffn_up(h, w_in, layer, n_rows, tn=512):
    k = h.shape[1]
    f = w_in.shape[2] // 2
    nj = f // tn
    tm = _big_row_tile(n_rows)
    ni = n_rows // tm
    nk = _n_chunks(k, ni)
    return pl.pallas_call(
        functools.partial(_swiglu_kernel, layer=layer, nj=nj, ni=ni),
        grid=(nj, ni),
        in_specs=[pl.BlockSpec((tm, k), lambda j, i: (i, 0)), _HBM],
        out_specs=pl.BlockSpec((tm, tn), lambda j, i: (i, j)),
        out_shape=jax.ShapeDtypeStruct((n_rows, f), BF16),
        scratch_shapes=_w_scratch(k, tn, nk) + _w_scratch(k, tn, nk),
        compiler_params=_params("arbitrary", "arbitrary"),
        name="ffn_up",
    )(h, w_in)


def _resid_kernel(a_ref, w_hbm, *refs, layer, nj, ni, coef, n_first):
    *x_refs, gate_ref, o_ref, wb, sb, semb = refs
    buf = _weight_pipeline([_WStream(w_hbm, layer, 0, wb, sb, semb)], nj, ni)
    y = _dot(a_ref[...], wb[buf])
    o_ref[...] = _pick_rows(x_refs, n_first, pl.program_id(1)) + (coef * gate_ref[0]) * y


def resid_matmul(a, w, layer, x, mods, gate_idx, coef, seg_of_tile, n_tiles, tn, tm=ROW_TILE):
    _, k, n = w.shape
    tn = min(tn, n)
    nj = n // tn
    assert ROW_TILE % tm == 0
    ni = n_tiles * (ROW_TILE // tm)
    nk = _n_chunks(k, ni)
    xs, x_specs, n_first = _row_sources(x, ni, tn, lambda j, i: j, tm)
    return pl.pallas_call(
        functools.partial(_resid_kernel, layer=layer, nj=nj, ni=ni, coef=coef, n_first=n_first),
        grid=(nj, ni),
        in_specs=[pl.BlockSpec((tm, k), lambda j, i: (i, 0)), _HBM] + x_specs + [
            pl.BlockSpec((1, 1, tn), lambda j, i: (seg_of_tile(i * tm // ROW_TILE), 0, gate_idx * nj + j)),
        ],
        out_specs=pl.BlockSpec((tm, tn), lambda j, i: (i, j)),
        out_shape=jax.ShapeDtypeStruct((n_tiles * ROW_TILE, n), F32),
        scratch_shapes=_w_scratch(k, tn, nk),
        compiler_params=_params("arbitrary", "arbitrary"),
        name="resid_matmul",
    )(a, w, *xs, mods)


def _proj_kernel(a_ref, w_hbm, o_ref, wb, sb, semb, *, layer, col0, nj, ni, gate):
    buf = _weight_pipeline([_WStream(w_hbm, layer, col0, wb, sb, semb)], nj, ni)
    for rows in _row_halves(a_ref):
        y = _dot(a_ref[rows, :], wb[buf])
        o_ref[rows, :] = (_sigmoid(y) if gate else y).astype(o_ref.dtype)


def proj_plain(h, w, layer, col0, width, n_rows, tn=512, gate=False):
    k = h.shape[1]
    nj = width // tn
    tm = _big_row_tile(n_rows)
    ni = n_rows // tm
    nk = _n_chunks(k, ni)
    return pl.pallas_call(
        functools.partial(_proj_kernel, layer=layer, col0=col0, nj=nj, ni=ni, gate=gate),
        grid=(nj, ni),
        in_specs=[pl.BlockSpec((tm, k), lambda j, i: (i, 0)), _HBM],
        out_specs=pl.BlockSpec((tm, tn), lambda j, i: (i, j)),
        out_shape=jax.ShapeDtypeStruct((n_rows, width), BF16),
        scratch_shapes=_w_scratch(k, tn, nk),
        compiler_params=_params("arbitrary", "arbitrary"),
        name="proj_plain",
    )(h, w)


def _swap_pairs(y):
    lane = lax.broadcasted_iota(jnp.int32, y.shape, 1)
    nxt = pltpu.roll(y, LANES - 1, 1)
    prv = pltpu.roll(y, 1, 1)
    return jnp.where(lane % 2 == 0, nxt, prv)


def _proj_qk_kernel(a_ref, w_hbm, gain_ref, *rest, layer, col0, nj, ni, head_dim, rope):
    if rope:
        cos_ref, sin_ref, o_ref, wb, sb, semb = rest
    else:
        o_ref, wb, sb, semb = rest
    buf = _weight_pipeline([_WStream(w_hbm, layer, col0, wb, sb, semb)], nj, ni)
    tn = wb.shape[2]
    for rows in _row_halves(a_ref, QK_SUB_BLOCKS):
        acc = _dot(a_ref[rows, :], wb[buf])
        for s in range(tn // LANES):
            sl = slice(s * LANES, (s + 1) * LANES)
            y = acc[:, sl]
            sq = y * y
            if head_dim == LANES:
                ms = jnp.mean(sq, axis=-1, keepdims=True)
            else:
                lane = lax.broadcasted_iota(jnp.int32, y.shape, 1)
                low = lane < head_dim
                s_low = jnp.sum(jnp.where(low, sq, 0.0), axis=-1, keepdims=True)
                s_high = jnp.sum(jnp.where(low, 0.0, sq), axis=-1, keepdims=True)
                ms = jnp.where(low, s_low, s_high) * (1.0 / head_dim)
            y = y * lax.rsqrt(ms + EPS) * gain_ref[:, sl]
            if rope:
                y = y * cos_ref[rows, :] + _swap_pairs(y) * sin_ref[rows, :]
            o_ref[rows, sl] = y.astype(o_ref.dtype)


def proj_qk(h, w, layer, col0, width, gain_row, head_dim, rope_tabs, n_rows, tn=512):
    k = h.shape[1]
    nj = width // tn
    tm = _big_row_tile(n_rows)
    ni = n_rows // tm
    nk = _n_chunks(k, ni)
    rope = rope_tabs is not None
    in_specs = [
        pl.BlockSpec((tm, k), lambda j, i: (i, 0)),
        _HBM,
        pl.BlockSpec((1, tn), lambda j, i: (0, j)),
    ]
    args = [h, w, gain_row]
    if rope:
        in_specs += [pl.BlockSpec((tm, LANES), lambda j, i: (i, 0))] * 2
        args += list(rope_tabs)
    return pl.pallas_call(
        functools.partial(_proj_qk_kernel, layer=layer, col0=col0, nj=nj, ni=ni, head_dim=head_dim, rope=rope),
        grid=(nj, ni),
        in_specs=in_specs,
        out_specs=pl.BlockSpec((tm, tn), lambda j, i: (i, j)),
        out_shape=jax.ShapeDtypeStruct((n_rows, width), BF16),
        scratch_shapes=_w_scratch(k, tn, nk),
        compiler_params=_params("arbitrary", "arbitrary"),
        name="proj_qk",
    )(*args)


def _merge_kernel(oa_ref, ob_ref, oc_ref, wa_hbm, wb_hbm, wc_hbm, ga_ref, gb_ref, gc_ref, o_ref,
                  wa, sa, sema, wb, sb, semb, wc, sc, semc, *, layer, nj, ni):
    buf = _weight_pipeline([_WStream(wa_hbm, layer, 0, wa, sa, sema), _WStream(wb_hbm, layer, 0, wb, sb, semb),
                            _WStream(wc_hbm, layer, 0, wc, sc, semc)], nj, ni)
    m = ga_ref[...].astype(F32) * _dot(oa_ref[...], wa[buf])
    m += gb_ref[...].astype(F32) * _dot(ob_ref[...], wb[buf])
    m += gc_ref[...].astype(F32) * _dot(oc_ref[...], wc[buf])
    o_ref[...] = m.astype(o_ref.dtype)


def gated_merge(oa, ob, oc, wa, wb, wc, layer, gates, n_tiles, tn=1024):
    d = wa.shape[2]
    tn = min(tn, d)
    nj = d // tn
    nk = min(_n_chunks(w.shape[1], n_tiles) for w in (wa, wb, wc))
    assert all(w.shape[1] % nk == 0 and (w.shape[1] // nk) % BF16_SUBLANES == 0 for w in (wa, wb, wc))
    act = lambda o: pl.BlockSpec((ROW_TILE, o.shape[1]), lambda j, i: (i, 0))
    gate = lambda b: pl.BlockSpec((ROW_TILE, tn), lambda j, i: (i, j + b * nj))
    return pl.pallas_call(
        functools.partial(_merge_kernel, layer=layer, nj=nj, ni=n_tiles),
        grid=(nj, n_tiles),
        in_specs=[act(oa), act(ob), act(oc), _HBM, _HBM, _HBM, gate(0), gate(1), gate(2)],
        out_specs=pl.BlockSpec((ROW_TILE, tn), lambda j, i: (i, j)),
        out_shape=jax.ShapeDtypeStruct((n_tiles * ROW_TILE, d), BF16),
        scratch_shapes=sum((_w_scratch(w.shape[1], tn, nk) for w in (wa, wb, wc)), []),
        compiler_params=_params("arbitrary", "arbitrary"),
        name="gated_merge",
    )(oa, ob, oc, wa, wb, wc, gates, gates, gates)


def _head_slices(n):
    return [slice(j * HEAD_DIM, (j + 1) * HEAD_DIM) for j in range(n)]


def _build_vaug(vaug_ref, vl_ref, vc_ref):
    seq = vl_ref.shape[0]
    for j, hs in enumerate(_head_slices(vaug_ref.shape[0])):
        vaug_ref[j, 0:seq, 0:HEAD_DIM] = vl_ref[:, hs]
        vaug_ref[j, seq:, 0:HEAD_DIM] = vc_ref[:, hs]
        vaug_ref[j, :, HEAD_DIM:] = jnp.ones((vaug_ref.shape[1], HEAD_DIM), vaug_ref.dtype)


def _attend(q, parts):
    acc = m_run = None
    for k, vaug, bias in parts:
        s = _dot_t(q, k)
        if bias is not None:
            s = s + bias
        m_new = jnp.max(s, axis=-1, keepdims=True)
        if acc is not None:
            m_new = jnp.maximum(m_run, m_new)
        o = _dot(jnp.exp2(s - m_new).astype(vaug.dtype), vaug)
        acc = o if acc is None else jnp.exp2(m_run - m_new) * acc + o
        m_run = m_new
    return acc[:, :HEAD_DIM] / acc[:, HEAD_DIM:]


def _key_parts(kl_ref, kc_ref, vaug_ref, latent, j, hs):
    seq = kl_ref.shape[0]
    parts = []
    if latent:
        for c in range(seq // KEY_CHUNK):
            sl = slice(c * KEY_CHUNK, (c + 1) * KEY_CHUNK)
            parts.append((kl_ref[sl, hs], vaug_ref[j, sl, :], None))
    parts.append((kc_ref[:, hs], vaug_ref[j, seq:, :], None))
    return parts


def _gqa_kernel(q_ref, kl_ref, kc_ref, vl_ref, vc_ref, o_ref, vaug_ref, *, group, n_lat, with_ctx):
    qt = pl.program_id(2)
    pl.when(qt == 0)(lambda: _build_vaug(vaug_ref, vl_ref, vc_ref))

    def run(latent):
        for j, hs in enumerate(_head_slices(vaug_ref.shape[0])):
            cols = [slice((j * group + g) * HEAD_DIM, (j * group + g + 1) * HEAD_DIM) for g in range(group)]
            q = jnp.concatenate([q_ref[:, c] for c in cols], axis=0)
            o = _attend(q, _key_parts(kl_ref, kc_ref, vaug_ref, latent, j, hs))
            for g, c in enumerate(cols):
                o_ref[:, c] = o[g * Q_TILE:(g + 1) * Q_TILE].astype(o_ref.dtype)

    if with_ctx:
        pl.when(qt < n_lat)(lambda: run(True))
        pl.when(qt >= n_lat)(lambda: run(False))
    else:
        run(True)


def _q_row_block(b, qt, n_lat, n_batch):
    return jnp.where(qt < n_lat, b * n_lat + qt, n_batch * n_lat + b)


def gqa_attention(q, k, v, n_batch, seq, ctx_len, group, with_ctx):
    rows = q.shape[0] if with_ctx else n_batch * seq
    kvh = k.shape[1] // HEAD_DIM
    n_lat = seq // Q_TILE
    ctx_blk0 = n_batch * seq // ctx_len
    nh = GQA_KV_HEADS if kvh % GQA_KV_HEADS == 0 else 1
    gw, kw = nh * group * HEAD_DIM, nh * HEAD_DIM
    qmap = lambda b, h, t: (_q_row_block(b, t, n_lat, n_batch), h)
    lat = pl.BlockSpec((seq, kw), lambda b, h, t: (b, h))
    ctx = pl.BlockSpec((ctx_len, kw), lambda b, h, t: (ctx_blk0 + b, h))
    return pl.pallas_call(
        functools.partial(_gqa_kernel, group=group, n_lat=n_lat, with_ctx=with_ctx),
        grid=(n_batch, kvh // nh, n_lat + int(with_ctx)),
        in_specs=[pl.BlockSpec((Q_TILE, gw), qmap), lat, ctx, lat, ctx],
        out_specs=pl.BlockSpec((Q_TILE, gw), qmap),
        out_shape=jax.ShapeDtypeStruct((rows, q.shape[1]), BF16),
        scratch_shapes=[pltpu.VMEM((nh, seq + ctx_len, 2 * HEAD_DIM), BF16)],
        compiler_params=_params("arbitrary", "arbitrary", "arbitrary"),
        name="gqa_attention",
    )(q, k, k, v, v)


def _diff_kernel(q_ref, kl_ref, kc_ref, vl_ref, vc_ref, lam_ref, g_ref, o_ref, vaug_ref, *,
                 lam_init, n_lat, with_ctx):
    qt = pl.program_id(2)
    pl.when(qt == 0)(lambda: _build_vaug(vaug_ref, vl_ref, vc_ref))
    lp = lam_ref[...]
    lam = (jnp.exp(jnp.sum(lp[0:1] * lp[1:2], axis=-1, keepdims=True))
           - jnp.exp(jnp.sum(lp[2:3] * lp[3:4], axis=-1, keepdims=True)) + lam_init)

    def run(latent):
        for j, hs in enumerate(_head_slices(vaug_ref.shape[0])):
            q = q_ref[:, hs]
            lane = lax.broadcasted_iota(jnp.int32, q.shape, 1)
            zero = jnp.zeros_like(q)
            q2 = jnp.concatenate([jnp.where(lane < C_QK_DIM, q, zero), jnp.where(lane < C_QK_DIM, zero, q)],
                                 axis=0)
            o2 = _attend(q2, _key_parts(kl_ref, kc_ref, vaug_ref, latent, j, hs))
            o = o2[:Q_TILE] - lam * o2[Q_TILE:]
            o = o * lax.rsqrt(jnp.mean(o * o, axis=-1, keepdims=True) + EPS) * g_ref[...]
            o_ref[:, hs] = (o * (1.0 - lam_init)).astype(o_ref.dtype)

    if with_ctx:
        pl.when(qt < n_lat)(lambda: run(True))
        pl.when(qt >= n_lat)(lambda: run(False))
    else:
        run(True)


def diff_attention(q, k, v, lam_params, subln, lam_init, n_batch, seq, ctx_len, with_ctx):
    rows = q.shape[0] if with_ctx else n_batch * seq
    n_heads = v.shape[1] // HEAD_DIM
    n_lat = seq // Q_TILE
    ctx_blk0 = n_batch * seq // ctx_len
    nh = DIFF_HEADS if n_heads % DIFF_HEADS == 0 else 1
    hw = nh * HEAD_DIM
    qmap = lambda b, h, t: (_q_row_block(b, t, n_lat, n_batch), h)
    lat = pl.BlockSpec((seq, hw), lambda b, h, t: (b, h))
    ctx = pl.BlockSpec((ctx_len, hw), lambda b, h, t: (ctx_blk0 + b, h))
    full = lambda a: pl.BlockSpec(a.shape, lambda b, h, t: (0,) * a.ndim)
    subln = subln.reshape(1, -1)
    return pl.pallas_call(
        functools.partial(_diff_kernel, lam_init=lam_init, n_lat=n_lat, with_ctx=with_ctx),
        grid=(n_batch, n_heads // nh, n_lat + int(with_ctx)),
        in_specs=[pl.BlockSpec((Q_TILE, hw), qmap), lat, ctx, lat, ctx, full(lam_params), full(subln)],
        out_specs=pl.BlockSpec((Q_TILE, hw), qmap),
        out_shape=jax.ShapeDtypeStruct((rows, v.shape[1]), BF16),
        scratch_shapes=[pltpu.VMEM((nh, seq + ctx_len, 2 * HEAD_DIM), BF16)],
        compiler_params=_params("arbitrary", "arbitrary", "arbitrary"),
        name="diff_attention",
    )(q, k, k, v, v, lam_params, subln)


def _na_plan(rows):
    kh = min(NA_KH, rows)
    band = min(NA_ROWS + kh, rows)
    patterns, types = [], []
    for blk in range(rows // NA_ROWS):
        r0 = blk * NA_ROWS
        bs = min(max(r0 - kh // 2, 0), rows - band)
        pat = []
        for qr in range(r0, r0 + NA_ROWS):
            rs = min(max(qr - kh // 2, 0), rows - kh)
            pat.append(tuple((kr - qr + kh - 1) if rs <= kr < rs + kh else -1 for kr in range(bs, bs + band)))
        pat = tuple(pat)
        if pat not in patterns:
            patterns.append(pat)
        types.append(patterns.index(pat))
    return band, np.asarray(types, np.int32), patterns


def _na_bias_kernel(rb_ref, o_ref, *, kh, patterns):
    h = pl.program_id(0)
    n_dc = 2 * NA_KW - 1
    base = h * ((2 * NA_KH - 1) * n_dc)
    qc = lax.broadcasted_iota(jnp.int32, (GRID_W, GRID_W), 0)
    kc = lax.broadcasted_iota(jnp.int32, (GRID_W, GRID_W), 1)
    dc = kc - qc
    cs = jnp.clip(qc - NA_KW // 2, 0, GRID_W - NA_KW)
    col_ok = (kc >= cs) & (kc < cs + NA_KW)
    neg = jnp.full((GRID_W, GRID_W), NEG_BIAS, F32)
    used = sorted({a for pat in patterns for row in pat for a in row if a >= 0})
    toep = {}
    for a in used:
        a_full = a + (NA_KH - kh)
        t = neg
        for b in range(n_dc):
            t = jnp.where(dc == b - (NA_KW - 1), rb_ref[base + a_full * n_dc + b] * LOG2E, t)
        toep[a] = jnp.where(col_ok, t, neg)
    for t_id, pat in enumerate(patterns):
        for qr, row in enumerate(pat):
            blocks = [toep[a] if a >= 0 else neg for a in row]
            o_ref[0, t_id, qr * GRID_W:(qr + 1) * GRID_W, :] = jnp.concatenate(blocks, axis=1)


def na_bias_tables(rel_bias, rows):
    n_heads = rel_bias.shape[0]
    kh = min(NA_KH, rows)
    band, _, patterns = _na_plan(rows)
    shape = (n_heads, len(patterns), Q_TILE, band * GRID_W)
    return pl.pallas_call(
        functools.partial(_na_bias_kernel, kh=kh, patterns=patterns),
        grid=(n_heads,),
        in_specs=[pl.BlockSpec(memory_space=pltpu.SMEM)],
        out_specs=pl.BlockSpec((1,) + shape[1:], lambda h: (h, 0, 0, 0)),
        out_shape=jax.ShapeDtypeStruct(shape, F32),
        compiler_params=_params("arbitrary"),
        name="na_bias_tables",
    )(rel_bias.reshape(-1))


def _na_kernel(types_ref, q_ref, kl_ref, kc_ref, vl_ref, vc_ref, bias_ref, o_ref, vaug_ref, *,
               rows, band, n_lat, with_ctx):
    del types_ref
    rt = pl.program_id(2)
    kh = min(NA_KH, rows)
    seq = kl_ref.shape[0]
    heads = _head_slices(NA_HEADS)
    pl.when(rt == 0)(lambda: _build_vaug(vaug_ref, vl_ref, vc_ref))
    ctx_part = lambda j, hs: (kc_ref[:, hs], vaug_ref[j, seq:, :], None)

    def latent():
        bs = jnp.clip(rt * NA_ROWS - kh // 2, 0, rows - band)
        band_sl = pl.ds(pl.multiple_of(bs * GRID_W, GRID_W), band * GRID_W)
        for j, hs in enumerate(heads):
            parts = [(kl_ref[band_sl, hs], vaug_ref[j, band_sl, :], bias_ref[j, 0]), ctx_part(j, hs)]
            o_ref[:, hs] = _attend(q_ref[:, hs], parts).astype(o_ref.dtype)

    def context():
        for j, hs in enumerate(heads):
            o_ref[:, hs] = _attend(q_ref[:, hs], [ctx_part(j, hs)]).astype(o_ref.dtype)

    if with_ctx:
        pl.when(rt < n_lat)(latent)
        pl.when(rt >= n_lat)(context)
    else:
        latent()


def neighbourhood_attention(q, k, v, bias_tabs, n_batch, seq, ctx_len, with_ctx):
    rows_tok = q.shape[0] if with_ctx else n_batch * seq
    n_heads = q.shape[1] // HEAD_DIM
    rows = seq // GRID_W
    band, types, _ = _na_plan(rows)
    n_lat = seq // Q_TILE
    ctx_blk0 = n_batch * seq // ctx_len
    types = jnp.asarray(np.concatenate([types, types[-1:]]))
    hw = NA_HEADS * HEAD_DIM
    assert n_heads % NA_HEADS == 0
    qmap = lambda b, h, t, ty: (_q_row_block(b, t, n_lat, n_batch), h)
    lat = pl.BlockSpec((seq, hw), lambda b, h, t, ty: (b, h))
    ctx = pl.BlockSpec((ctx_len, hw), lambda b, h, t, ty: (ctx_blk0 + b, h))
    bias = pl.BlockSpec((NA_HEADS, 1, Q_TILE, band * GRID_W), lambda b, h, t, ty: (h, ty[t], 0, 0))
    return pl.pallas_call(
        functools.partial(_na_kernel, rows=rows, band=band, n_lat=n_lat, with_ctx=with_ctx),
        grid_spec=pltpu.PrefetchScalarGridSpec(
            num_scalar_prefetch=1,
            grid=(n_batch, n_heads // NA_HEADS, n_lat + int(with_ctx)),
            in_specs=[pl.BlockSpec((Q_TILE, hw), qmap), lat, ctx, lat, ctx, bias],
            out_specs=pl.BlockSpec((Q_TILE, hw), qmap),
            scratch_shapes=[pltpu.VMEM((NA_HEADS, seq + ctx_len, 2 * HEAD_DIM), BF16)],
        ),
        out_shape=jax.ShapeDtypeStruct((rows_tok, q.shape[1]), BF16),
        compiler_params=_params("arbitrary", "arbitrary", "arbitrary"),
        name="neighbourhood_attention",
    )(types, q, k, k, v, v, bias_tabs)


def _rope_tables(seq, dim, n_batch, n_ctx_rows):
    t = jnp.arange(seq, dtype=jnp.int32)
    row = (t // GRID_W).astype(F32)
    col = (t % GRID_W).astype(F32)
    n_pairs = dim // 4
    inv = ROPE_THETA ** (-jnp.arange(n_pairs, dtype=F32) / n_pairs)
    ang = jnp.concatenate([row[:, None] * inv, col[:, None] * inv], axis=-1)
    cos = jnp.repeat(jnp.cos(ang), 2, axis=-1)
    sin = jnp.stack([-jnp.sin(ang), jnp.sin(ang)], axis=-1).reshape(seq, dim)
    reps = LANES // dim
    cos, sin = jnp.tile(cos, (n_batch, reps)), jnp.tile(sin, (n_batch, reps))
    cos = jnp.concatenate([cos, jnp.ones((n_ctx_rows, LANES), F32)], axis=0)
    sin = jnp.concatenate([sin, jnp.zeros((n_ctx_rows, LANES), F32)], axis=0)
    return cos, sin


def kernel(x, c, ctx, c_ctx, w_ada, b_ada, norm_g, ffn1_in, ffn1_out, ffn2_in, ffn2_out, w_in,
           qk_gain_a, qk_gain_b, qk_gain_c, na_rel_bias, diff_lambda, diff_subln,
           w_br_a, w_br_b, w_br_c, w_out):
    n_batch, seq, d = x.shape
    ctx_len = ctx.shape[1]
    depth = w_ada.shape[0]
    a_q_w, b_w, c_w = w_br_a.shape[1], w_br_b.shape[1], w_br_c.shape[1]
    a_kv_w = (w_in.shape[2] - a_q_w - 3 * b_w - 3 * c_w - 3 * d) // 2
    group = a_q_w // a_kv_w
    assert ctx_len == Q_TILE and seq % ROW_TILE == 0 and (n_batch * ctx_len) % ROW_TILE == 0
    assert seq % KEY_CHUNK == 0 and n_batch + 1 <= MOD_ROWS
    n_lat_rows = n_batch * seq
    n_ctx_rows = n_batch * ctx_len
    lat_tiles = n_lat_rows // ROW_TILE
    all_tiles = lat_tiles + n_ctx_rows // ROW_TILE
    n_all_rows = n_lat_rows + n_ctx_rows
    tiles_per_batch = seq // ROW_TILE
    seg_of_tile = lambda i: jnp.minimum(i // tiles_per_batch, n_batch)

    sizes = (a_q_w, a_kv_w, a_kv_w, b_w, b_w, b_w, c_w, c_w, c_w, 3 * d)
    offs = [int(o) for o in np.cumsum((0,) + sizes)]

    cvec = jnp.concatenate([c, c_ctx[None], jnp.zeros((MOD_ROWS - n_batch - 1, d), F32)], axis=0)
    mods_all = ada_modulation(cvec, w_ada, b_ada)

    rope_a = _rope_tables(seq, HEAD_DIM, n_batch, n_ctx_rows)
    rope_c = _rope_tables(seq, C_QK_DIM, n_batch, n_ctx_rows)
    rows = seq // GRID_W

    xs = (x.reshape(n_lat_rows, d), ctx.reshape(n_ctx_rows, d))

    w1i, w1o, w2i, w2o = ffn1_in, ffn1_out, ffn2_in, ffn2_out
    wp, wa, wb, wc, wo = w_in, w_br_a, w_br_b, w_br_c, w_out

    def gain_row(gain, width, scale):
        return jnp.tile(gain * scale, width // gain.shape[0]).reshape(1, width)

    sa, sc = HEAD_DIM ** -0.5 * LOG2E, C_QK_DIM ** -0.5 * LOG2E

    for l in range(depth):
        last = l == depth - 1
        with_ctx = not last
        lam_init = 0.8 - 0.6 * math.exp(-0.3 * l)
        mods = mods_all[l].reshape(MOD_ROWS, 1, N_MOD * d)

        h = norm_mod(xs, norm_g[l, 0], mods, 0, 1, seg_of_tile, all_tiles)
        a = ffn_up(h, w1i, l, n_all_rows)
        xs = resid_matmul(a, w1o, l, xs, mods, 2, 0.5, seg_of_tile, all_tiles, tn=1024, tm=FFN_DOWN_ROWS)

        h = norm_mod(xs, norm_g[l, 1], mods, 3, 4, seg_of_tile, all_tiles)
        qa = proj_qk(h, wp, l, offs[0], a_q_w, gain_row(qk_gain_a[l, 0], a_q_w, sa), HEAD_DIM, rope_a, n_all_rows)
        ka = proj_qk(h, wp, l, offs[1], a_kv_w, gain_row(qk_gain_a[l, 1], a_kv_w, 1.0), HEAD_DIM, rope_a, n_all_rows)
        va = proj_plain(h, wp, l, offs[2], a_kv_w, n_all_rows)
        qb = proj_qk(h, wp, l, offs[3], b_w, gain_row(qk_gain_b[l, 0], b_w, sa), HEAD_DIM, None, n_all_rows)
        kb = proj_qk(h, wp, l, offs[4], b_w, gain_row(qk_gain_b[l, 1], b_w, 1.0), HEAD_DIM, None, n_all_rows)
        vb = proj_plain(h, wp, l, offs[5], b_w, n_all_rows)
        qc = proj_qk(h, wp, l, offs[6], c_w, gain_row(qk_gain_c[l, 0], c_w, sc), C_QK_DIM, rope_c, n_all_rows)
        kc = proj_qk(h, wp, l, offs[7], c_w, gain_row(qk_gain_c[l, 1], c_w, 1.0), C_QK_DIM, rope_c, n_all_rows)
        vc = proj_plain(h, wp, l, offs[8], c_w, n_all_rows)
        mix_tiles = all_tiles if with_ctx else lat_tiles
        mix_rows = mix_tiles * ROW_TILE
        gates = proj_plain(h, wp, l, offs[9], 3 * d, mix_rows, tn=min(1024, d), gate=True)

        o_a = gqa_attention(qa, ka, va, n_batch, seq, ctx_len, group, with_ctx)
        bias_tabs = na_bias_tables(na_rel_bias[l], rows)
        o_b = neighbourhood_attention(qb, kb, vb, bias_tabs, n_batch, seq, ctx_len, with_ctx)
        o_c = diff_attention(qc, kc, vc, diff_lambda[l], diff_subln[l], lam_init, n_batch, seq, ctx_len, with_ctx)

        m = gated_merge(o_a, o_b, o_c, wa, wb, wc, l, gates, mix_tiles)
        xs = resid_matmul(m, wo, l, xs, mods, 5, 1.0, seg_of_tile, mix_tiles, tn=1024)

        h = norm_mod(xs, norm_g[l, 2], mods, 6, 7, seg_of_tile, mix_tiles)
        a = ffn_up(h, w2i, l, mix_rows)
        xs = resid_matmul(a, w2o, l, xs, mods, 8, 0.5, seg_of_tile, mix_tiles, tn=1024, tm=FFN_DOWN_ROWS)

    return xs.reshape(n_batch, seq, d)
```
